```python
import jax
import jax.numpy as jnp
from jax import lax
import numpy as np

D_MODEL = 2048
BATCH = 1
SEQ = 16384
DEPTH = 2

N_MIXERS = 2
GRID_W = 64
N_MEM = 256
RMS_EPS = 1e-6
GN_EPS = 1e-6

RET_HEADS = 8
RET_QK_DIM = D_MODEL // RET_HEADS
RET_V_DIM = 2 * D_MODEL // RET_HEADS
RET_CHUNK = 128
ROPE_BASE = 10000.0

NA_HEADS = 16
NA_HEAD_DIM = D_MODEL // NA_HEADS
NA_WIN_ROWS = 8
NA_WIN_COLS = 16

XA_HEADS = 4
XA_HEAD_DIM = 128
XA_INNER = XA_HEADS * XA_HEAD_DIM

FFN_DIM = 11 * D_MODEL // 4
N_EXPERTS = 8
MOE_TOP_K = 2
EXPERT_DIM = 7 * D_MODEL // 2
MOE_BLOCK = 128

N_EVEN = (DEPTH + 1) // 2
N_ODD = DEPTH // 2

kernel_name = 'hybrid_retention_natten_moe_encoder'


def _rms_norm(x, gain):
    xf = x.astype(jnp.float32)
    y = xf * lax.rsqrt(jnp.mean(xf * xf, axis=-1, keepdims=True) + RMS_EPS)
    return (y * gain.astype(jnp.float32)).astype(x.dtype)


def _rotary(x):
    s, d = x.shape[1], x.shape[-1]
    inv_freq = jnp.power(ROPE_BASE, -jnp.arange(0, d, 2, dtype=jnp.float32) / d)
    ang = jnp.arange(s, dtype=jnp.float32)[:, None] * inv_freq[None, :]
    cos = jnp.cos(ang)[None, :, None, :]
    sin = jnp.sin(ang)[None, :, None, :]
    x1, x2 = x[..., : d // 2], x[..., d // 2:]
    return jnp.concatenate([x1 * cos - x2 * sin, x2 * cos + x1 * sin], axis=-1)


def _retention_one_direction(q, k, v, log_decay, strict):
    b, s, h, dk = q.shape
    dv = v.shape[-1]
    c = RET_CHUNK
    nc = s // c
    idx = jnp.arange(c, dtype=jnp.float32)
    rel = idx[:, None] - idx[None, :]
    keep = (rel > 0) if strict else (rel >= 0)
    decay_in = jnp.where(keep[None], jnp.exp(log_decay[:, None, None] * jnp.maximum(rel, 0.0)[None]), 0.0)
    xi = jnp.exp(log_decay[:, None] * (idx + 1.0)[None])
    zeta = jnp.exp(log_decay[:, None] * (c - 1.0 - idx)[None])
    decay_chunk = jnp.exp(log_decay * c)

    def to_chunks(t):
        return t.reshape(b, nc, c, h, t.shape[-1]).transpose(1, 0, 3, 2, 4)

    def step(state, inp):
        qi, ki, vi = inp
        scores = jnp.einsum('bhqd,bhkd->bhqk', qi, ki) * decay_in
        inner = jnp.einsum('bhqk,bhkv->bhqv', scores, vi)
        cross = jnp.einsum('bhqd,bhdv->bhqv', qi * xi[None, :, :, None], state)
        state = decay_chunk[None, :, None, None] * state + jnp.einsum('bhkd,bhkv->bhdv', ki * zeta[None, :, :, None], vi)
        return state, inner + cross

    state0 = jnp.zeros((b, h, dk, dv), jnp.float32)
    _, out = lax.scan(step, state0, (to_chunks(q), to_chunks(k), to_chunks(v)))
    return out.transpose(1, 0, 3, 2, 4).reshape(b, s, h, dv)


def _retention_mixer(h, w_in, w_out, decay_logit, gn_gain):
    b, s, _ = h.shape
    qk_w = RET_HEADS * RET_QK_DIM
    v_w = RET_HEADS * RET_V_DIM
    proj = h @ w_in
    q = proj[..., :qk_w].astype(jnp.float32).reshape(b, s, RET_HEADS, RET_QK_DIM)
    k = proj[..., qk_w:2 * qk_w].astype(jnp.float32).reshape(b, s, RET_HEADS, RET_QK_DIM)
    v = proj[..., 2 * qk_w:2 * qk_w + v_w].astype(jnp.float32).reshape(b, s, RET_HEADS, RET_V_DIM)
    g = proj[..., 2 * qk_w + v_w:].astype(jnp.float32)
    q = _rotary(q)
    k = _rotary(k) * (RET_QK_DIM ** -0.5)
    log_decay = jax.nn.log_sigmoid(decay_logit.astype(jnp.float32))
    y_fwd = _retention_one_direction(q, k, v, log_decay[0], strict=False)
    y_bwd = jnp.flip(_retention_one_direction(jnp.flip(q, 1), jnp.flip(k, 1), jnp.flip(v, 1), log_decay[1], strict=True), 1)
    y = y_fwd + y_bwd
    mu = jnp.mean(y, axis=-1, keepdims=True)
    var = jnp.mean(jnp.square(y - mu), axis=-1, keepdims=True)
    y = ((y - mu) * lax.rsqrt(var + GN_EPS)).reshape(b, s, v_w) * gn_gain.astype(jnp.float32)
    y = jax.nn.silu(g) * y
    return y.astype(h.dtype) @ w_out


def _neighbourhood_attention(h, w_in, w_out, rpb):
    b, s, _ = h.shape
    rows = s // GRID_W
    wr = min(NA_WIN_ROWS, rows)
    wc = NA_WIN_COLS
    n_keys = wr * wc
    width = NA_HEADS * NA_HEAD_DIM
    qkv = h @ w_in
    q = qkv[..., :width].reshape(b, s, NA_HEADS, NA_HEAD_DIM)
    k = qkv[..., width:2 * width].reshape(b, s, NA_HEADS, NA_HEAD_DIM)
    v = qkv[..., 2 * width:].reshape(b, s, NA_HEADS, NA_HEAD_DIM)
    cols = jnp.arange(GRID_W)
    col_start = jnp.clip(cols - wc // 2, 0, GRID_W - wc)
    key_cols = col_start[:, None] + jnp.arange(wc)[None, :]
    dc = key_cols - cols[:, None]
    scale = NA_HEAD_DIM ** -0.5

    def row_block(r):
        row_start = jnp.clip(r - wr // 2, 0, rows - wr)
        key_rows = row_start + jnp.arange(wr)
        tok = (key_rows[None, :, None] * GRID_W + key_cols[:, None, :]).reshape(GRID_W, n_keys)
        dr = key_rows - r
        bias = rpb[:, dr[None, :, None] + NA_WIN_ROWS - 1, dc[:, None, :] + NA_WIN_COLS - 1]
        bias = bias.reshape(NA_HEADS, GRID_W, n_keys).astype(jnp.float32)
        q_blk = lax.dynamic_slice_in_dim(q, r * GRID_W, GRID_W, axis=1)
        k_blk = jnp.take(k, tok, axis=1)
        v_blk = jnp.take(v, tok, axis=1)
        scores = jnp.einsum('bqhd,bqkhd->bhqk', q_blk, k_blk).astype(jnp.float32) * scale + bias[None]
        p = jax.nn.softmax(scores, axis=-1)
        return jnp.einsum('bhqk,bqkhd->bqhd', p.astype(v.dtype), v_blk)

    out = lax.map(row_block, jnp.arange(rows))
    out = out.transpose(1, 0, 2, 3, 4).reshape(b, s, width)
    return out @ w_out


def _memory_cross_attention(h, mem_k, mem_v, w_q, w_out):
    b, s, _ = h.shape
    q = (h @ w_q).reshape(b, s, XA_HEADS, XA_HEAD_DIM)
    scores = jnp.einsum('bqhd,bkhd->bhqk', q, mem_k).astype(jnp.float32) * (XA_HEAD_DIM ** -0.5)
    p = jax.nn.softmax(scores, axis=-1)
    o = jnp.einsum('bhqk,bkhd->bqhd', p.astype(mem_v.dtype), mem_v)
    return o.reshape(b, s, XA_INNER) @ w_out


def _swiglu(h, w_gate, w_up, w_down):
    return (jax.nn.silu(h @ w_gate) * (h @ w_up)) @ w_down


def _moe_swiglu(h, w_router, w_gate, w_up, w_down):
    b, s, d = h.shape
    t = b * s
    xt = h.reshape(t, d)
    logits = (xt @ w_router).astype(jnp.float32)
    top_val, top_idx = lax.top_k(logits, MOE_TOP_K)
    gates = jax.nn.softmax(top_val, axis=-1)
    flat_e = top_idx.reshape(-1)
    flat_tok = jnp.repeat(jnp.arange(t, dtype=jnp.int32), MOE_TOP_K)
    flat_w = gates.reshape(-1)
    order = jnp.argsort(flat_e)
    se, stok, sw = flat_e[order], flat_tok[order], flat_w[order]
    counts = jnp.zeros((N_EXPERTS,), jnp.int32).at[flat_e].add(1)
    padded = ((counts + MOE_BLOCK - 1) // MOE_BLOCK) * MOE_BLOCK
    start = jnp.cumsum(counts) - counts
    pend = jnp.cumsum(padded)
    pstart = pend - padded
    n_assign = t * MOE_TOP_K
    pos = jnp.arange(n_assign) - start[se] + pstart[se]
    n_blocks = n_assign // MOE_BLOCK + N_EXPERTS
    n_pad = n_blocks * MOE_BLOCK
    buf_tok = jnp.zeros((n_pad,), jnp.int32).at[pos].set(stok)
    buf_w = jnp.zeros((n_pad,), jnp.float32).at[pos].set(sw)
    block_e = jnp.clip(jnp.searchsorted(pend, jnp.arange(n_blocks) * MOE_BLOCK, side='right'), 0, N_EXPERTS - 1)

    def expert_block(args):
        tok, wts, e = args
        xb = xt[tok]
        y = (jax.nn.silu(xb @ w_gate[e]) * (xb @ w_up[e])) @ w_down[e]
        return y * wts[:, None].astype(y.dtype)

    y = lax.map(expert_block, (buf_tok.reshape(n_blocks, MOE_BLOCK), buf_w.reshape(n_blocks, MOE_BLOCK), block_e))
    out = jnp.zeros((t, d), y.dtype).at[buf_tok].add(y.reshape(n_pad, d))
    return out.astype(h.dtype).reshape(b, s, d)


def setup_inputs(seed: int = 0) -> dict:
    key = jax.random.key(seed)
    ks = jax.random.split(key, 24)
    f32 = jnp.float32

    def w(k, shape, fan_in):
        return jax.random.normal(k, shape, f32) * (fan_in ** -0.5)

    def gain(k, shape):
        return 1.0 + 0.02 * jax.random.normal(k, shape, f32)

    ret_in_cols = 2 * RET_HEADS * RET_QK_DIM + 2 * RET_HEADS * RET_V_DIM
    ret_v = RET_HEADS * RET_V_DIM
    heads = jnp.arange(RET_HEADS, dtype=f32)
    base = jnp.stack([jnp.log(jnp.exp2(5.0 + heads) - 1.0), jnp.log(jnp.exp2(5.5 + heads) - 1.0)])
    ret_decay_logit = base[None] + 0.1 * jax.random.normal(ks[0], (N_EVEN, 2, RET_HEADS), f32)
    na_width = NA_HEADS * NA_HEAD_DIM

    x = jax.random.normal(ks[1], (BATCH, SEQ, D_MODEL), f32)
    mem = jax.random.normal(ks[2], (BATCH, N_MEM, D_MODEL), f32)
    mem_norm_gain = gain(ks[3], (D_MODEL,))
    mem_w_kv = w(ks[4], (D_MODEL, 2 * XA_INNER), D_MODEL)
    norm_gain = gain(ks[5], (DEPTH, 3, D_MODEL))
    ret_w_in = w(ks[6], (N_EVEN, D_MODEL, ret_in_cols), D_MODEL)
    ret_w_out = w(ks[7], (N_EVEN, ret_v, D_MODEL), ret_v)
    ret_gn_gain = gain(ks[8], (N_EVEN, ret_v))
    ffn_w_gate = w(ks[9], (N_EVEN, D_MODEL, FFN_DIM), D_MODEL)
    ffn_w_up = w(ks[10], (N_EVEN, D_MODEL, FFN_DIM), D_MODEL)
    ffn_w_down = w(ks[11], (N_EVEN, FFN_DIM, D_MODEL), FFN_DIM)
    na_w_in = w(ks[12], (N_ODD, D_MODEL, 3 * na_width), D_MODEL)
    na_w_out = w(ks[13], (N_ODD, na_width, D_MODEL), na_width)
    na_rpb = 0.1 * jax.random.normal(ks[14], (N_ODD, NA_HEADS, 2 * NA_WIN_ROWS - 1, 2 * NA_WIN_COLS - 1), f32)
    moe_router = w(ks[15], (N_ODD, D_MODEL, N_EXPERTS), D_MODEL)
    moe_w_gate = w(ks[16], (N_ODD, N_EXPERTS, D_MODEL, EXPERT_DIM), D_MODEL)
    moe_w_up = w(ks[17], (N_ODD, N_EXPERTS, D_MODEL, EXPERT_DIM), D_MODEL)
    moe_w_down = w(ks[18], (N_ODD, N_EXPERTS, EXPERT_DIM, D_MODEL), EXPERT_DIM)
    xa_w_q = w(ks[19], (DEPTH, D_MODEL, XA_INNER), D_MODEL)
    xa_w_out = w(ks[20], (DEPTH, XA_INNER, D_MODEL), XA_INNER)
    final_norm_gain = gain(ks[21], (D_MODEL,))
    return {'x': x, 'mem': mem, 'mem_norm_gain': mem_norm_gain, 'mem_w_kv': mem_w_kv, 'norm_gain': norm_gain, 'ret_w_in': ret_w_in, 'ret_w_out': ret_w_out, 'ret_decay_logit': ret_decay_logit, 'ret_gn_gain': ret_gn_gain, 'ffn_w_gate': ffn_w_gate, 'ffn_w_up': ffn_w_up, 'ffn_w_down': ffn_w_down, 'na_w_in': na_w_in, 'na_w_out': na_w_out, 'na_rpb': na_rpb, 'moe_router': moe_router, 'moe_w_gate': moe_w_gate, 'moe_w_up': moe_w_up, 'moe_w_down': moe_w_down, 'xa_w_q': xa_w_q, 'xa_w_out': xa_w_out, 'final_norm_gain': final_norm_gain}


def reference(x, mem, mem_norm_gain, mem_w_kv, norm_gain, ret_w_in, ret_w_out, ret_decay_logit, ret_gn_gain, ffn_w_gate, ffn_w_up, ffn_w_down, na_w_in, na_w_out, na_rpb, moe_router, moe_w_gate, moe_w_up, moe_w_down, xa_w_q, xa_w_out, final_norm_gain):
    b, m, _ = mem.shape
    mem_kv = _rms_norm(mem, mem_norm_gain) @ mem_w_kv
    mem_k = mem_kv[..., :XA_INNER].reshape(b, m, XA_HEADS, XA_HEAD_DIM)
    mem_v = mem_kv[..., XA_INNER:].reshape(b, m, XA_HEADS, XA_HEAD_DIM)
    h = x
    for i in range(DEPTH):
        j = i // 2
        hn = _rms_norm(h, norm_gain[i, 0])
        if i % N_MIXERS == 0:
            h = h + _retention_mixer(hn, ret_w_in[j], ret_w_out[j], ret_decay_logit[j], ret_gn_gain[j])
        else:
            h = h + _neighbourhood_attention(hn, na_w_in[j], na_w_out[j], na_rpb[j])
        h = h + _memory_cross_attention(_rms_norm(h, norm_gain[i, 1]), mem_k, mem_v, xa_w_q[i], xa_w_out[i])
        hn = _rms_norm(h, norm_gain[i, 2])
        if i % 2 == 0:
            h = h + _swiglu(hn, ffn_w_gate[j], ffn_w_up[j], ffn_w_down[j])
        else:
            h = h + _moe_swiglu(hn, moe_router[j], moe_w_gate[j], moe_w_up[j], moe_w_down[j])
    return _rms_norm(h, final_norm_gain)
```

```python
import functools

import numpy as np
import jax
import jax.numpy as jnp
from jax import lax
from jax.experimental import pallas as pl
from jax.experimental.pallas import tpu as pltpu

F32 = jnp.float32
BF16 = jnp.bfloat16

GRID_W = 64
RMS_EPS = 1e-6
GN_EPS = 1e-6
RET_HEADS = 8
ROPE_BASE = 10000.0
NA_HEADS = 16
NA_WIN_ROWS = 8
NA_WIN_COLS = 16
XA_HEADS = 4
XA_HEAD_DIM = 128
MOE_TOP_K = 2

V7X_VMEM_BYTES = 64 * 1024 * 1024
VMEM_LIMIT = V7X_VMEM_BYTES * 3 // 4
LANES = 128

RET_CHUNK = 256
NA_QROWS = 8
NA_KROWS = 2 * NA_QROWS
MOE_ROWS = 512
GATHER_ROWS = 256
NEG = -1e30


def _params(sem):
    return pltpu.CompilerParams(dimension_semantics=sem, vmem_limit_bytes=VMEM_LIMIT)


def _silu(g):
    return g / (1.0 + jnp.exp(-g))


def _rms(x, gain):
    return x * lax.rsqrt(jnp.mean(x * x, axis=-1, keepdims=True) + RMS_EPS) * gain


def _rmsnorm_kernel(x_ref, g_ref, o_ref):
    o_ref[...] = _rms(x_ref[...], g_ref[...]).astype(o_ref.dtype)


def _rmsnorm(x, gain, out_dtype, tm=512):
    m, d = x.shape
    tm = min(tm, m)
    return pl.pallas_call(
        _rmsnorm_kernel,
        grid=(m // tm,),
        in_specs=[pl.BlockSpec((tm, d), lambda i: (i, 0)), pl.BlockSpec((1, d), lambda i: (0, 0))],
        out_specs=pl.BlockSpec((tm, d), lambda i: (i, 0)),
        out_shape=jax.ShapeDtypeStruct((m, d), out_dtype),
        compiler_params=_params(("parallel",)),
        name="rmsnorm",
    )(x, gain.reshape(1, d))


def _mm_kernel(a_ref, w_ref, o_ref):
    o_ref[...] = jnp.dot(a_ref[...], w_ref[...], preferred_element_type=F32).astype(o_ref.dtype)


def _mm_res_kernel(a_ref, w_ref, r_ref, o_ref):
    o_ref[...] = (r_ref[...] + jnp.dot(a_ref[...], w_ref[...], preferred_element_type=F32)).astype(o_ref.dtype)


def _matmul(a, w, out_dtype, residual=None, name="matmul"):
    m, k = a.shape
    n = w.shape[1]
    tm = min(m, 1024 if k <= 2048 else 512)
    tn = min(n, 1024)
    in_specs = [pl.BlockSpec((tm, k), lambda i, j: (i, 0)), pl.BlockSpec((k, tn), lambda i, j: (0, j))]
    args = [a, w]
    body = _mm_kernel
    if residual is not None:
        in_specs.append(pl.BlockSpec((tm, tn), lambda i, j: (i, j)))
        args.append(residual)
        body = _mm_res_kernel
    return pl.pallas_call(
        body,
        grid=(m // tm, n // tn),
        in_specs=in_specs,
        out_specs=pl.BlockSpec((tm, tn), lambda i, j: (i, j)),
        out_shape=jax.ShapeDtypeStruct((m, n), out_dtype),
        compiler_params=_params(("parallel", "parallel")),
        name=name,
    )(*args)


def _ret_kernel(ld_ref, q_ref, k_ref, v_ref, cos_ref, sin_ref, *rest, backward, dk):
    if backward:
        yf_ref, g_ref, gain_ref, o_ref, state_ref = rest
    else:
        o_ref, state_ref = rest
    h = pl.program_id(0)
    c = pl.program_id(1)

    @pl.when(c == 0)
    def _():
        state_ref[...] = jnp.zeros_like(state_ref)

    ld = ld_ref[h]
    n = q_ref.shape[0]
    half = dk // 2
    cos = cos_ref[...]
    sin = sin_ref[...]

    def rot(x):
        x = x.astype(F32)
        x1, x2 = x[:, :half], x[:, half:]
        return jnp.concatenate([x1 * cos - x2 * sin, x2 * cos + x1 * sin], axis=1)

    q = rot(q_ref[...])
    k = rot(k_ref[...]) * (dk ** -0.5)
    v = v_ref[...]

    i = lax.broadcasted_iota(jnp.int32, (n, n), 0)
    j = lax.broadcasted_iota(jnp.int32, (n, n), 1)
    pos = lax.broadcasted_iota(jnp.int32, (n, 1), 0).astype(F32)
    if backward:
        rel = (j - i).astype(F32)
        keep = j > i
        xi = jnp.exp(ld * (n - pos))
        zeta = jnp.exp(ld * pos)
    else:
        rel = (i - j).astype(F32)
        keep = i >= j
        xi = jnp.exp(ld * (pos + 1.0))
        zeta = jnp.exp(ld * (n - 1.0 - pos))
    decay = jnp.where(keep, jnp.exp(ld * jnp.maximum(rel, 0.0)), 0.0)

    scores = lax.dot_general(q.astype(BF16), k.astype(BF16), (((1,), (1,)), ((), ())),
                             preferred_element_type=F32) * decay
    inner = jnp.dot(scores.astype(BF16), v, preferred_element_type=F32)
    state = state_ref[...]
    cross = jnp.dot((q * xi).astype(BF16), state.astype(BF16), preferred_element_type=F32)
    state_ref[...] = jnp.exp(ld * n) * state + lax.dot_general(
        (k * zeta).astype(BF16), v, (((0,), (0,)), ((), ())), preferred_element_type=F32)
    y = inner + cross
    if backward:
        y = y + yf_ref[...]
        mu = jnp.mean(y, axis=-1, keepdims=True)
        d = y - mu
        var = jnp.mean(d * d, axis=-1, keepdims=True)
        yn = d * lax.rsqrt(var + GN_EPS) * gain_ref[...]
        o_ref[...] = (_silu(g_ref[...].astype(F32)) * yn).astype(o_ref.dtype)
    else:
        o_ref[...] = y


def _retention(proj, cos, sin, log_decay, gn_gain):
    t = proj.shape[0]
    hh = RET_HEADS
    dv = gn_gain.shape[0] // hh
    dk = (proj.shape[1] - 2 * hh * dv) // (2 * hh)
    c = min(RET_CHUNK, t)
    nc = t // c
    kq = hh * dk // dk
    kv = 2 * hh * dk // dv
    kg = kv + hh

    def specs(order):
        return [
            pl.BlockSpec(memory_space=pltpu.SMEM),
            pl.BlockSpec((c, dk), lambda h, i: (order(i), h)),
            pl.BlockSpec((c, dk), lambda h, i: (order(i), kq + h)),
            pl.BlockSpec((c, dv), lambda h, i: (order(i), kv + h)),
            pl.BlockSpec((c, dk // 2), lambda h, i: (order(i), 0)),
            pl.BlockSpec((c, dk // 2), lambda h, i: (order(i), 0)),
        ]

    fwd = lambda i: i
    bwd = lambda i: nc - 1 - i
    y_fwd = pl.pallas_call(
        functools.partial(_ret_kernel, backward=False, dk=dk),
        grid=(hh, nc),
        in_specs=specs(fwd),
        out_specs=pl.BlockSpec((c, dv), lambda h, i: (i, h)),
        out_shape=jax.ShapeDtypeStruct((t, hh * dv), F32),
        scratch_shapes=[pltpu.VMEM((dk, dv), F32)],
        compiler_params=_params(("parallel", "arbitrary")),
        name="retention_fwd",
    )(log_decay[0], proj, proj, proj, cos, sin)
    return pl.pallas_call(
        functools.partial(_ret_kernel, backward=True, dk=dk),
        grid=(hh, nc),
        in_specs=specs(bwd) + [
            pl.BlockSpec((c, dv), lambda h, i: (bwd(i), h)),
            pl.BlockSpec((c, dv), lambda h, i: (bwd(i), kg + h)),
            pl.BlockSpec((1, dv), lambda h, i: (0, h)),
        ],
        out_specs=pl.BlockSpec((c, dv), lambda h, i: (bwd(i), h)),
        out_shape=jax.ShapeDtypeStruct((t, hh * dv), BF16),
        scratch_shapes=[pltpu.VMEM((dk, dv), F32)],
        compiler_params=_params(("parallel", "arbitrary")),
        name="retention_bwd",
    )(log_decay[1], proj, proj, proj, cos, sin, y_fwd, proj, gn_gain.reshape(1, -1))


def _rotary_tables(t, dk):
    inv_freq = jnp.power(ROPE_BASE, -jnp.arange(0, dk, 2, dtype=F32) / dk)
    ang = jnp.arange(t, dtype=F32)[:, None] * inv_freq[None, :]
    return jnp.cos(ang), jnp.sin(ang)


def _xattn_kernel(h_ref, g_ref, wq_ref, kv_ref, wo_ref, o_ref):
    x = h_ref[...]
    hn = _rms(x, g_ref[...]).astype(BF16)
    q = jnp.dot(hn, wq_ref[...], preferred_element_type=F32)
    inner = XA_HEADS * XA_HEAD_DIM
    outs = []
    for hd in range(XA_HEADS):
        lo = hd * XA_HEAD_DIM
        qh = q[:, lo:lo + XA_HEAD_DIM].astype(BF16)
        kh = kv_ref[:, lo:lo + XA_HEAD_DIM]
        vh = kv_ref[:, inner + lo:inner + lo + XA_HEAD_DIM]
        s = lax.dot_general(qh, kh, (((1,), (1,)), ((), ())), preferred_element_type=F32) * (XA_HEAD_DIM ** -0.5)
        p = jnp.exp(s - jnp.max(s, axis=-1, keepdims=True))
        p = p / jnp.sum(p, axis=-1, keepdims=True)
        outs.append(jnp.dot(p.astype(BF16), vh, preferred_element_type=F32))
    o = jnp.concatenate(outs, axis=1).astype(BF16)
    o_ref[...] = x + jnp.dot(o, wo_ref[...], preferred_element_type=F32)


def _xattn(h, gain, mem_kv, w_q, w_out, tm=512):
    t, d = h.shape
    tm = min(tm, t)
    full = lambda a: pl.BlockSpec(a.shape, lambda i: (0,) * a.ndim)
    gain = gain.reshape(1, d)
    return pl.pallas_call(
        _xattn_kernel,
        grid=(t // tm,),
        in_specs=[pl.BlockSpec((tm, d), lambda i: (i, 0)), full(gain), full(w_q), full(mem_kv), full(w_out)],
        out_specs=pl.BlockSpec((tm, d), lambda i: (i, 0)),
        out_shape=jax.ShapeDtypeStruct((t, d), F32),
        compiler_params=_params(("parallel",)),
        name="mem_xattn",
    )(h, gain, w_q, mem_kv, w_out)


def _ffn_kernel(x_ref, wg_ref, wu_ref, wd_ref, r_ref, o_ref):
    f = pl.program_id(1)
    x = x_ref[...]
    g = jnp.dot(x, wg_ref[...], preferred_element_type=F32)
    u = jnp.dot(x, wu_ref[...], preferred_element_type=F32)
    part = jnp.dot((_silu(g) * u).astype(BF16), wd_ref[...], preferred_element_type=F32)

    @pl.when(f == 0)
    def _():
        o_ref[...] = r_ref[...] + part

    @pl.when(f > 0)
    def _():
        o_ref[...] += part


def _ffn(x, w_gate, w_up, w_down, residual, tm=512, tf=512):
    t, d = x.shape
    f = w_gate.shape[1]
    tm = min(tm, t)
    return pl.pallas_call(
        _ffn_kernel,
        grid=(t // tm, f // tf),
        in_specs=[
            pl.BlockSpec((tm, d), lambda i, j: (i, 0)),
            pl.BlockSpec((d, tf), lambda i, j: (0, j)),
            pl.BlockSpec((d, tf), lambda i, j: (0, j)),
            pl.BlockSpec((tf, d), lambda i, j: (j, 0)),
            pl.BlockSpec((tm, d), lambda i, j: (i, 0)),
        ],
        out_specs=pl.BlockSpec((tm, d), lambda i, j: (i, 0)),
        out_shape=jax.ShapeDtypeStruct((t, d), F32),
        compiler_params=_params(("parallel", "arbitrary")),
        name="swiglu",
    )(x, w_gate, w_up, w_down, residual)


def _na_bias_tables(rpb, rows):
    heads = rpb.shape[0]
    w = GRID_W
    nd = 2 * NA_WIN_ROWS - 1
    ncol = 2 * NA_WIN_COLS - 1
    qc = np.arange(w)[:, None]
    kc = np.arange(w)[None, :]
    cs = np.clip(qc - NA_WIN_COLS // 2, 0, w - NA_WIN_COLS)
    col_ok = (kc >= cs) & (kc < cs + NA_WIN_COLS)
    dc = np.clip(kc - qc + NA_WIN_COLS - 1, 0, ncol - 1)
    sel = (dc.reshape(-1)[None, :] == np.arange(ncol)[:, None]).astype(np.float32)
    tab = jnp.dot(rpb.reshape(heads * nd, ncol), jnp.asarray(sel), precision=lax.Precision.HIGHEST)
    tab = jnp.where(jnp.asarray(col_ok.reshape(-1))[None, :], tab, NEG).reshape(heads, nd, w, w)
    tab = jnp.concatenate([tab, jnp.full((heads, 1, w, w), NEG, F32)], axis=1)
    nb = rows // NA_QROWS
    variants = []
    for b in (0, 1, nb - 1):
        ks = int(np.clip(b * NA_QROWS - NA_WIN_ROWS // 2, 0, rows - NA_KROWS))
        r = b * NA_QROWS + np.arange(NA_QROWS)[:, None]
        kr = ks + np.arange(NA_KROWS)[None, :]
        rs = np.clip(r - NA_WIN_ROWS // 2, 0, rows - NA_WIN_ROWS)
        ok = (kr >= rs) & (kr < rs + NA_WIN_ROWS)
        didx = np.where(ok, kr - r + NA_WIN_ROWS - 1, nd)
        blk = tab[:, didx]
        variants.append(blk.transpose(0, 1, 3, 2, 4).reshape(heads, NA_QROWS * w, NA_KROWS * w))
    return jnp.stack(variants)


def _na_kernel(q_ref, *rest, nkb, scale):
    k_refs = rest[:nkb]
    v_refs = rest[nkb:2 * nkb]
    b_ref, o_ref = rest[2 * nkb:]
    q = q_ref[...]
    kw = k_refs[0].shape[0]
    s = [lax.dot_general(q, k_refs[j][...], (((1,), (1,)), ((), ())), preferred_element_type=F32) * scale
         + b_ref[:, j * kw:(j + 1) * kw] for j in range(nkb)]
    m = functools.reduce(jnp.maximum, [jnp.max(x, axis=-1, keepdims=True) for x in s])
    p = [jnp.exp(x - m) for x in s]
    inv = 1.0 / functools.reduce(jnp.add, [jnp.sum(x, axis=-1, keepdims=True) for x in p])
    acc = None
    for j in range(nkb):
        part = jnp.dot((p[j] * inv).astype(BF16), v_refs[j][...], preferred_element_type=F32)
        acc = part if acc is None else acc + part
    o_ref[...] = acc.astype(o_ref.dtype)


def _neighbourhood_attention(qkv, bias):
    t = qkv.shape[0]
    heads = NA_HEADS
    dh = qkv.shape[1] // (3 * heads)
    rows = t // GRID_W
    nb = rows // NA_QROWS
    tq = NA_QROWS * GRID_W
    kw = tq // 2
    nkb = NA_KROWS * GRID_W // kw
    max_kb = (rows - NA_KROWS) * GRID_W // kw

    def kstart(b):
        return jnp.clip(2 * b - 1, 0, max_kb)

    def variant(b):
        return jnp.where(b == 0, 0, jnp.where(b == nb - 1, 2, 1))

    kspecs = [pl.BlockSpec((kw, dh), functools.partial(lambda h, b, j: (kstart(b) + j, heads + h), j=j))
              for j in range(nkb)]
    vspecs = [pl.BlockSpec((kw, dh), functools.partial(lambda h, b, j: (kstart(b) + j, 2 * heads + h), j=j))
              for j in range(nkb)]
    return pl.pallas_call(
        functools.partial(_na_kernel, nkb=nkb, scale=dh ** -0.5),
        grid=(heads, nb),
        in_specs=[pl.BlockSpec((tq, dh), lambda h, b: (b, h))] + kspecs + vspecs
        + [pl.BlockSpec((None, None, tq, nkb * kw), lambda h, b: (variant(b), h, 0, 0))],
        out_specs=pl.BlockSpec((tq, dh), lambda h, b: (b, h)),
        out_shape=jax.ShapeDtypeStruct((t, heads * dh), BF16),
        compiler_params=_params(("parallel", "arbitrary")),
        name="neighbourhood_attention",
    )(qkv, *([qkv] * (2 * nkb)), bias)


def _norm_router_kernel(x_ref, g_ref, wr_ref, hn_ref, route_ref, *, n_experts):
    hn = _rms(x_ref[...], g_ref[...])
    hn_ref[...] = hn
    logits = jnp.dot(hn, wr_ref[...], precision=lax.Precision.HIGHEST, preferred_element_type=F32)
    lane = lax.broadcasted_iota(jnp.int32, logits.shape, 1)
    lg = jnp.where(lane < n_experts, logits, -jnp.inf)
    m1 = jnp.max(lg, axis=-1, keepdims=True)
    i1 = jnp.min(jnp.where(lg == m1, lane, LANES), axis=-1, keepdims=True)
    lg2 = jnp.where(lane == i1, -jnp.inf, lg)
    m2 = jnp.max(lg2, axis=-1, keepdims=True)
    i2 = jnp.min(jnp.where(lg2 == m2, lane, LANES), axis=-1, keepdims=True)
    e = jnp.exp(m2 - m1)
    g1 = 1.0 / (1.0 + e)
    g2 = e / (1.0 + e)
    route_ref[...] = jnp.where(lane == 0, i1.astype(F32), jnp.where(lane == 1, i2.astype(F32),
                               jnp.where(lane == 2, g1, jnp.where(lane == 3, g2, 0.0))))


def _norm_router(h, gain, w_router, tm=512):
    t, d = h.shape
    tm = min(tm, t)
    n_experts = w_router.shape[1]
    wr = jnp.zeros((d, LANES), F32).at[:, :n_experts].set(w_router)
    return pl.pallas_call(
        functools.partial(_norm_router_kernel, n_experts=n_experts),
        grid=(t // tm,),
        in_specs=[pl.BlockSpec((tm, d), lambda i: (i, 0)), pl.BlockSpec((1, d), lambda i: (0, 0)),
                  pl.BlockSpec((d, LANES), lambda i: (0, 0))],
        out_specs=[pl.BlockSpec((tm, d), lambda i: (i, 0)), pl.BlockSpec((tm, LANES), lambda i: (i, 0))],
        out_shape=[jax.ShapeDtypeStruct((t, d), F32), jax.ShapeDtypeStruct((t, LANES), F32)],
        compiler_params=_params(("parallel",)),
        name="norm_router",
    )(h, gain.reshape(1, d), wr)


def _row_copy(src_hbm, dst_vmem, sem, src_row, dst_row):
    return pltpu.make_async_copy(src_hbm.at[pl.ds(src_row, 1)], dst_vmem.at[pl.ds(dst_row, 1)], sem)


def _gather_kernel(tok_ref, x_hbm, o_ref, sem):
    rows = o_ref.shape[0]

    def issue(i, carry):
        _row_copy(x_hbm, o_ref, sem, tok_ref[0, 0, i], i).start()
        return carry

    lax.fori_loop(0, rows, issue, 0)

    def drain(i, carry):
        _row_copy(x_hbm, o_ref, sem, 0, i).wait()
        return carry

    lax.fori_loop(0, rows, drain, 0)


def _gather_rows(x, tok_sorted):
    n_pad = tok_sorted.shape[0]
    d = x.shape[1]
    r = GATHER_ROWS
    return pl.pallas_call(
        _gather_kernel,
        grid=(n_pad // r,),
        in_specs=[pl.BlockSpec((1, 1, r), lambda i: (i, 0, 0), memory_space=pltpu.SMEM),
                  pl.BlockSpec(memory_space=pl.ANY)],
        out_specs=pl.BlockSpec((r, d), lambda i: (i, 0)),
        out_shape=jax.ShapeDtypeStruct((n_pad, d), F32),
        scratch_shapes=[pltpu.SemaphoreType.DMA(())],
        compiler_params=_params(("arbitrary",)),
        name="moe_gather",
    )(tok_sorted.reshape(n_pad // r, 1, r), x)


def _expert_kernel(be_ref, nv_ref, x_ref, wg_ref, wu_ref, wd_ref, o_ref, xb_ref):
    m = pl.program_id(0)
    f = pl.program_id(1)
    valid = m < nv_ref[0]

    @pl.when(valid)
    def _():
        @pl.when(f == 0)
        def _():
            xb_ref[...] = x_ref[...].astype(BF16)

        x = xb_ref[...]
        g = jnp.dot(x, wg_ref[...], preferred_element_type=F32)
        u = jnp.dot(x, wu_ref[...], preferred_element_type=F32)
        part = jnp.dot((_silu(g) * u).astype(BF16), wd_ref[...], preferred_element_type=F32)

        @pl.when(f == 0)
        def _():
            o_ref[...] = part

        @pl.when(f > 0)
        def _():
            o_ref[...] += part

    @pl.when(jnp.logical_and(jnp.logical_not(valid), f == 0))
    def _():
        o_ref[...] = jnp.zeros_like(o_ref)


def _experts(xs, block_e, n_valid, w_gate, w_up, w_down, tf=512):
    n_pad, d = xs.shape
    fdim = w_gate.shape[2]
    tm = MOE_ROWS
    nb = n_pad // tm
    nf = fdim // tf

    def fidx(m, f, nv):
        return jnp.where(m < nv[0], f, nf - 1)

    grid_spec = pltpu.PrefetchScalarGridSpec(
        num_scalar_prefetch=2,
        grid=(nb, nf),
        in_specs=[
            pl.BlockSpec((tm, d), lambda m, f, be, nv: (jnp.minimum(m, nv[0] - 1), 0)),
            pl.BlockSpec((None, d, tf), lambda m, f, be, nv: (be[m], 0, fidx(m, f, nv))),
            pl.BlockSpec((None, d, tf), lambda m, f, be, nv: (be[m], 0, fidx(m, f, nv))),
            pl.BlockSpec((None, tf, d), lambda m, f, be, nv: (be[m], fidx(m, f, nv), 0)),
        ],
        out_specs=pl.BlockSpec((tm, d), lambda m, f, be, nv: (m, 0)),
        scratch_shapes=[pltpu.VMEM((tm, d), BF16)],
    )
    return pl.pallas_call(
        _expert_kernel,
        grid_spec=grid_spec,
        out_shape=jax.ShapeDtypeStruct((n_pad, d), F32),
        compiler_params=_params(("arbitrary", "arbitrary")),
        name="moe_experts",
    )(block_e, n_valid, xs, w_gate, w_up, w_down)


def _combine_kernel(pos_ref, h_ref, route_ref, gain_ref, y_hbm, o_ref, y0_ref, y1_ref, sem, *, final_norm):
    rows = h_ref.shape[0]

    def issue(i, carry):
        _row_copy(y_hbm, y0_ref, sem.at[0], pos_ref[0, 0, 2 * i], i).start()
        _row_copy(y_hbm, y1_ref, sem.at[1], pos_ref[0, 0, 2 * i + 1], i).start()
        return carry

    lax.fori_loop(0, rows, issue, 0)

    def drain(i, carry):
        _row_copy(y_hbm, y0_ref, sem.at[0], 0, i).wait()
        _row_copy(y_hbm, y1_ref, sem.at[1], 0, i).wait()
        return carry

    lax.fori_loop(0, rows, drain, 0)
    g1 = route_ref[:, 2:3]
    g2 = route_ref[:, 3:4]
    hh = h_ref[...] + (g1 * y0_ref[...] + g2 * y1_ref[...])
    o_ref[...] = _rms(hh, gain_ref[...]) if final_norm else hh


def _combine(h, y, pos, route, final_gain, final_norm):
    t, d = h.shape
    r = GATHER_ROWS
    return pl.pallas_call(
        functools.partial(_combine_kernel, final_norm=final_norm),
        grid=(t // r,),
        in_specs=[pl.BlockSpec((1, 1, MOE_TOP_K * r), lambda i: (i, 0, 0), memory_space=pltpu.SMEM),
                  pl.BlockSpec((r, d), lambda i: (i, 0)),
                  pl.BlockSpec((r, LANES), lambda i: (i, 0)),
                  pl.BlockSpec((1, d), lambda i: (0, 0)),
                  pl.BlockSpec(memory_space=pl.ANY)],
        out_specs=pl.BlockSpec((r, d), lambda i: (i, 0)),
        out_shape=jax.ShapeDtypeStruct((t, d), F32),
        scratch_shapes=[pltpu.VMEM((r, d), F32), pltpu.VMEM((r, d), F32), pltpu.SemaphoreType.DMA((2,))],
        compiler_params=_params(("arbitrary",)),
        name="moe_combine_norm",
    )(pos.reshape(t // r, 1, MOE_TOP_K * r), h, route, final_gain.reshape(1, d), y)


def _routing_plan(route, n_experts):
    t = route.shape[0]
    n_assign = t * MOE_TOP_K
    e_flat = route[:, :MOE_TOP_K].astype(jnp.int32).reshape(-1)
    onehot = (e_flat[:, None] == jnp.arange(n_experts, dtype=jnp.int32)[None, :]).astype(jnp.int32)
    csum = jnp.cumsum(onehot, axis=0)
    rank = jnp.take_along_axis(csum, e_flat[:, None], axis=1)[:, 0] - 1
    counts = csum[-1]
    padded = (counts + MOE_ROWS - 1) // MOE_ROWS * MOE_ROWS
    pend = jnp.cumsum(padded)
    pos = (pend - padded)[e_flat] + rank
    n_blocks = n_assign // MOE_ROWS + n_experts
    tok = jnp.arange(n_assign, dtype=jnp.int32) // MOE_TOP_K
    tok_sorted = jnp.zeros((n_blocks * MOE_ROWS,), jnp.int32).at[pos].set(tok)
    n_valid = (pend[-1] // MOE_ROWS).astype(jnp.int32)
    blk = jnp.arange(n_blocks, dtype=jnp.int32)
    block_e = jnp.clip(jnp.searchsorted(pend, blk * MOE_ROWS, side="right"), 0, n_experts - 1).astype(jnp.int32)
    block_e = jnp.where(blk < n_valid, block_e, block_e[n_valid - 1])
    return pos.astype(jnp.int32), tok_sorted, block_e, n_valid.reshape(1)


def kernel(x, mem, mem_norm_gain, mem_w_kv, norm_gain, ret_w_in, ret_w_out, ret_decay_logit, ret_gn_gain, ffn_w_gate, ffn_w_up, ffn_w_down, na_w_in, na_w_out, na_rpb, moe_router, moe_w_gate, moe_w_up, moe_w_down, xa_w_q, xa_w_out, final_norm_gain):
    b, s, d = x.shape
    assert b == 1
    depth = norm_gain.shape[0]
    bf = lambda w: w.astype(BF16)

    mem_kv = _matmul(_rmsnorm(mem[0], mem_norm_gain, BF16), bf(mem_w_kv), BF16, name="mem_kv")
    dk = (ret_w_in.shape[2] - 2 * ret_gn_gain.shape[1]) // (2 * RET_HEADS)
    cos, sin = _rotary_tables(s, dk)

    h = x[0]
    for i in range(depth):
        j = i // 2
        hn = _rmsnorm(h, norm_gain[i, 0], BF16)
        if i % 2 == 0:
            proj = _matmul(hn, bf(ret_w_in[j]), BF16, name="ret_in")
            log_decay = jax.nn.log_sigmoid(ret_decay_logit[j].astype(F32))
            yg = _retention(proj, cos, sin, log_decay, ret_gn_gain[j])
            h = _matmul(yg, bf(ret_w_out[j]), F32, residual=h, name="ret_out")
        else:
            qkv = _matmul(hn, bf(na_w_in[j]), BF16, name="na_in")
            o = _neighbourhood_attention(qkv, _na_bias_tables(na_rpb[j], s // GRID_W))
            h = _matmul(o, bf(na_w_out[j]), F32, residual=h, name="na_out")
        h = _xattn(h, norm_gain[i, 1], mem_kv, bf(xa_w_q[i]), bf(xa_w_out[i]))
        if i % 2 == 0:
            hn = _rmsnorm(h, norm_gain[i, 2], BF16)
            h = _ffn(hn, bf(ffn_w_gate[j]), bf(ffn_w_up[j]), bf(ffn_w_down[j]), h)
        else:
            hn32, route = _norm_router(h, norm_gain[i, 2], moe_router[j])
            pos, tok_sorted, block_e, n_valid = _routing_plan(route, moe_router.shape[2])
            xs = _gather_rows(hn32, tok_sorted)
            y = _experts(xs, block_e, n_valid, bf(moe_w_gate[j]), bf(moe_w_up[j]), bf(moe_w_down[j]))
            h = _combine(h, y, pos, route, final_norm_gain, final_norm=(i == depth - 1))
    if depth % 2 == 1:
        h = _rmsnorm(h, final_norm_gain, F32)
    return h[None]
```

```python
import functools

import numpy as np
import jax
import jax.numpy as jnp
from jax import lax
from jax.experimental import pallas as pl
from jax.experimental.pallas import tpu as pltpu

F32 = jnp.float32
BF16 = jnp.bfloat16

GRID_W = 64
RMS_EPS = 1e-6
GN_EPS = 1e-6
RET_HEADS = 8
ROPE_BASE = 10000.0
NA_HEADS = 16
NA_WIN_ROWS = 8
NA_WIN_COLS = 16
XA_HEADS = 4
XA_HEAD_DIM = 128
MOE_TOP_K = 2

V7X_VMEM_BYTES = 64 * 1024 * 1024
VMEM_LIMIT = V7X_VMEM_BYTES * 3 // 4
LANES = 128

RET_CHUNK = 256
RET_STEP_ROWS = 1024
NA_QROWS = 8
NA_KROWS = 2 * NA_QROWS
NA_HEADS_PER_STEP = 2
MOE_ROWS = 512
GATHER_ROWS = 256
NEG = -1e30


def _params(sem):
    return pltpu.CompilerParams(dimension_semantics=sem, vmem_limit_bytes=VMEM_LIMIT)


def _silu(g):
    return g / (1.0 + jnp.exp(-g))


def _rms(x, gain):
    return x * lax.rsqrt(jnp.mean(x * x, axis=-1, keepdims=True) + RMS_EPS) * gain


def _rmsnorm_kernel(x_ref, g_ref, o_ref):
    o_ref[...] = _rms(x_ref[...], g_ref[...]).astype(o_ref.dtype)


def _rmsnorm(x, gain, out_dtype, tm=512):
    m, d = x.shape
    tm = min(tm, m)
    return pl.pallas_call(
        _rmsnorm_kernel,
        grid=(m // tm,),
        in_specs=[pl.BlockSpec((tm, d), lambda i: (i, 0)), pl.BlockSpec((1, d), lambda i: (0, 0))],
        out_specs=pl.BlockSpec((tm, d), lambda i: (i, 0)),
        out_shape=jax.ShapeDtypeStruct((m, d), out_dtype),
        compiler_params=_params(("parallel",)),
        name="rmsnorm",
    )(x, gain.reshape(1, d))


def _mm_kernel(a_ref, w_ref, o_ref):
    o_ref[...] = jnp.dot(a_ref[...], w_ref[...], preferred_element_type=F32).astype(o_ref.dtype)


def _mm_res_kernel(a_ref, w_ref, r_ref, o_ref):
    o_ref[...] = (r_ref[...] + jnp.dot(a_ref[...], w_ref[...], preferred_element_type=F32)).astype(o_ref.dtype)


def _mm_rotary_kernel(a_ref, w_ref, cos_ref, sin_ref, o_ref, *, dk, first_k_tile):
    acc = jnp.dot(a_ref[...], w_ref[...], preferred_element_type=F32)
    cos = cos_ref[...]
    sin = sin_ref[...]
    half = dk // 2
    scale = jnp.where(pl.program_id(1) >= first_k_tile, dk ** -0.5, 1.0)
    for lo in range(0, acc.shape[1], dk):
        x1 = acc[:, lo:lo + half]
        x2 = acc[:, lo + half:lo + dk]
        o_ref[:, lo:lo + half] = ((x1 * cos - x2 * sin) * scale).astype(o_ref.dtype)
        o_ref[:, lo + half:lo + dk] = ((x2 * cos + x1 * sin) * scale).astype(o_ref.dtype)


def _matmul(a, w, out_dtype, residual=None, name="matmul", cols=None, rotary=None):
    m, k = a.shape
    c0, n = cols if cols is not None else (0, w.shape[1])
    tm = min(m, 1024 if k <= 2048 else 512)
    tn = min(n, 1024)
    j0 = c0 // tn
    in_specs = [pl.BlockSpec((tm, k), lambda i, j: (i, 0)), pl.BlockSpec((k, tn), lambda i, j: (0, j0 + j))]
    args = [a, w]
    body = _mm_kernel
    if residual is not None:
        in_specs.append(pl.BlockSpec((tm, tn), lambda i, j: (i, j)))
        args.append(residual)
        body = _mm_res_kernel
    if rotary is not None:
        cos, sin, dk, n_q_cols = rotary
        in_specs += [pl.BlockSpec((tm, dk // 2), lambda i, j: (i, 0))] * 2
        args += [cos, sin]
        body = functools.partial(_mm_rotary_kernel, dk=dk, first_k_tile=n_q_cols // tn)
    return pl.pallas_call(
        body,
        grid=(m // tm, n // tn),
        in_specs=in_specs,
        out_specs=pl.BlockSpec((tm, tn), lambda i, j: (i, j)),
        out_shape=jax.ShapeDtypeStruct((m, n), out_dtype),
        compiler_params=_params(("parallel", "parallel")),
        name=name,
    )(*args)


def _ret_kernel(ld_ref, q_ref, k_ref, v_ref, *rest, backward, n):
    if backward:
        yf_ref, g_ref, gain_ref, o_ref, state_ref = rest
    else:
        o_ref, state_ref = rest
    h = pl.program_id(0)

    @pl.when(pl.program_id(1) == 0)
    def _():
        state_ref[...] = jnp.zeros_like(state_ref)

    ld = ld_ref[h]
    n_sub = q_ref.shape[0] // n

    i = lax.broadcasted_iota(jnp.int32, (n, n), 0)
    j = lax.broadcasted_iota(jnp.int32, (n, n), 1)
    pos = lax.broadcasted_iota(jnp.int32, (n, 1), 0).astype(F32)
    if backward:
        rel = (j - i).astype(F32)
        keep = j > i
        xi = jnp.exp(ld * (n - pos))
        zeta = jnp.exp(ld * pos)
    else:
        rel = (i - j).astype(F32)
        keep = i >= j
        xi = jnp.exp(ld * (pos + 1.0))
        zeta = jnp.exp(ld * (n - 1.0 - pos))
    decay = jnp.where(keep, jnp.exp(ld * jnp.maximum(rel, 0.0)), 0.0)
    chunk_decay = jnp.exp(ld * n)

    for s in range(n_sub):
        rows = slice((n_sub - 1 - s) * n, (n_sub - s) * n) if backward else slice(s * n, (s + 1) * n)
        q = q_ref[rows, :]
        k = k_ref[rows, :]
        v = v_ref[rows, :]
        scores = lax.dot_general(q, k, (((1,), (1,)), ((), ())), preferred_element_type=F32) * decay
        inner = jnp.dot(scores.astype(BF16), v, preferred_element_type=F32)
        state = state_ref[...]
        cross = jnp.dot((q.astype(F32) * xi).astype(BF16), state.astype(BF16), preferred_element_type=F32)
        state_ref[...] = chunk_decay * state + lax.dot_general(
            (k.astype(F32) * zeta).astype(BF16), v, (((0,), (0,)), ((), ())), preferred_element_type=F32)
        y = inner + cross
        if backward:
            y = y + yf_ref[rows, :]
            mu = jnp.mean(y, axis=-1, keepdims=True)
            d = y - mu
            var = jnp.mean(d * d, axis=-1, keepdims=True)
            yn = d * lax.rsqrt(var + GN_EPS) * gain_ref[...]
            o_ref[rows, :] = (_silu(g_ref[rows, :].astype(F32)) * yn).astype(o_ref.dtype)
        else:
            o_ref[rows, :] = y


def _retention(qk, vg, log_decay, gn_gain):
    t = qk.shape[0]
    hh = RET_HEADS
    dk = qk.shape[1] // (2 * hh)
    dv = vg.shape[1] // (2 * hh)
    c = min(RET_CHUNK, t)
    rows = min(RET_STEP_ROWS, t)
    ns = t // rows

    def specs(order):
        return [
            pl.BlockSpec(memory_space=pltpu.SMEM),
            pl.BlockSpec((rows, dk), lambda h, i: (order(i), h)),
            pl.BlockSpec((rows, dk), lambda h, i: (order(i), hh + h)),
            pl.BlockSpec((rows, dv), lambda h, i: (order(i), h)),
        ]

    fwd = lambda i: i
    bwd = lambda i: ns - 1 - i
    y_fwd = pl.pallas_call(
        functools.partial(_ret_kernel, backward=False, n=c),
        grid=(hh, ns),
        in_specs=specs(fwd),
        out_specs=pl.BlockSpec((rows, dv), lambda h, i: (i, h)),
        out_shape=jax.ShapeDtypeStruct((t, hh * dv), F32),
        scratch_shapes=[pltpu.VMEM((dk, dv), F32)],
        compiler_params=_params(("parallel", "arbitrary")),
        name="retention_fwd",
    )(log_decay[0], qk, qk, vg)
    return pl.pallas_call(
        functools.partial(_ret_kernel, backward=True, n=c),
        grid=(hh, ns),
        in_specs=specs(bwd) + [
            pl.BlockSpec((rows, dv), lambda h, i: (bwd(i), h)),
            pl.BlockSpec((rows, dv), lambda h, i: (bwd(i), hh + h)),
            pl.BlockSpec((1, dv), lambda h, i: (0, h)),
        ],
        out_specs=pl.BlockSpec((rows, dv), lambda h, i: (bwd(i), h)),
        out_shape=jax.ShapeDtypeStruct((t, hh * dv), BF16),
        scratch_shapes=[pltpu.VMEM((dk, dv), F32)],
        compiler_params=_params(("parallel", "arbitrary")),
        name="retention_bwd",
    )(log_decay[1], qk, qk, vg, y_fwd, vg, gn_gain.reshape(1, -1))


def _rotary_tables(t, dk):
    inv_freq = jnp.power(ROPE_BASE, -jnp.arange(0, dk, 2, dtype=F32) / dk)
    ang = jnp.arange(t, dtype=F32)[:, None] * inv_freq[None, :]
    return jnp.cos(ang), jnp.sin(ang)


def _xattn_kernel(h_ref, g_ref, wq_ref, kv_ref, wo_ref, o_ref):
    x = h_ref[...]
    hn = _rms(x, g_ref[...]).astype(BF16)
    q = jnp.dot(hn, wq_ref[...], preferred_element_type=F32)
    inner = XA_HEADS * XA_HEAD_DIM
    outs = []
    for hd in range(XA_HEADS):
        lo = hd * XA_HEAD_DIM
        qh = q[:, lo:lo + XA_HEAD_DIM].astype(BF16)
        kh = kv_ref[:, lo:lo + XA_HEAD_DIM]
        vh = kv_ref[:, inner + lo:inner + lo + XA_HEAD_DIM]
        s = lax.dot_general(qh, kh, (((1,), (1,)), ((), ())), preferred_element_type=F32) * (XA_HEAD_DIM ** -0.5)
        p = jnp.exp(s - jnp.max(s, axis=-1, keepdims=True))
        p = p / jnp.sum(p, axis=-1, keepdims=True)
        outs.append(jnp.dot(p.astype(BF16), vh, preferred_element_type=F32))
    o = jnp.concatenate(outs, axis=1).astype(BF16)
    o_ref[...] = x + jnp.dot(o, wo_ref[...], preferred_element_type=F32)


def _xattn(h, gain, mem_kv, w_q, w_out, tm=512):
    t, d = h.shape
    tm = min(tm, t)
    full = lambda a: pl.BlockSpec(a.shape, lambda i: (0,) * a.ndim)
    gain = gain.reshape(1, d)
    return pl.pallas_call(
        _xattn_kernel,
        grid=(t // tm,),
        in_specs=[pl.BlockSpec((tm, d), lambda i: (i, 0)), full(gain), full(w_q), full(mem_kv), full(w_out)],
        out_specs=pl.BlockSpec((tm, d), lambda i: (i, 0)),
        out_shape=jax.ShapeDtypeStruct((t, d), F32),
        compiler_params=_params(("parallel",)),
        name="mem_xattn",
    )(h, gain, w_q, mem_kv, w_out)


def _ffn_kernel(x_ref, wg_ref, wu_ref, wd_ref, r_ref, o_ref):
    @pl.when(pl.program_id(1) == 0)
    def _():
        o_ref[...] = r_ref[...]

    x = x_ref[...]
    g = jnp.dot(x, wg_ref[...], preferred_element_type=F32)
    u = jnp.dot(x, wu_ref[...], preferred_element_type=F32)
    o_ref[...] += jnp.dot((_silu(g) * u).astype(BF16), wd_ref[...], preferred_element_type=F32)


def _ffn(x, w_gate, w_up, w_down, residual, tm=512, tf=512):
    t, d = x.shape
    f = w_gate.shape[1]
    tm = min(tm, t)
    return pl.pallas_call(
        _ffn_kernel,
        grid=(t // tm, f // tf),
        in_specs=[
            pl.BlockSpec((tm, d), lambda i, j: (i, 0)),
            pl.BlockSpec((d, tf), lambda i, j: (0, j)),
            pl.BlockSpec((d, tf), lambda i, j: (0, j)),
            pl.BlockSpec((tf, d), lambda i, j: (j, 0)),
            pl.BlockSpec((tm, d), lambda i, j: (i, 0)),
        ],
        out_specs=pl.BlockSpec((tm, d), lambda i, j: (i, 0)),
        out_shape=jax.ShapeDtypeStruct((t, d), F32),
        compiler_params=_params(("parallel", "arbitrary")),
        name="swiglu",
    )(x, w_gate, w_up, w_down, residual)


def _na_bias_tables(rpb, rows):
    heads = rpb.shape[0]
    w = GRID_W
    nd = 2 * NA_WIN_ROWS - 1
    ncol = 2 * NA_WIN_COLS - 1
    qc = np.arange(w)[:, None]
    kc = np.arange(w)[None, :]
    cs = np.clip(qc - NA_WIN_COLS // 2, 0, w - NA_WIN_COLS)
    col_ok = (kc >= cs) & (kc < cs + NA_WIN_COLS)
    dc = np.clip(kc - qc + NA_WIN_COLS - 1, 0, ncol - 1)
    sel = (dc.reshape(-1)[None, :] == np.arange(ncol)[:, None]).astype(np.float32)
    tab = jnp.dot(rpb.reshape(heads * nd, ncol), jnp.asarray(sel), precision=lax.Precision.HIGHEST)
    tab = jnp.where(jnp.asarray(col_ok.reshape(-1))[None, :], tab, NEG).reshape(heads, nd, w, w)
    tab = jnp.concatenate([tab, jnp.full((heads, 1, w, w), NEG, F32)], axis=1)
    nb = rows // NA_QROWS
    variants = []
    for b in (0, 1, nb - 1):
        ks = int(np.clip(b * NA_QROWS - NA_WIN_ROWS // 2, 0, rows - NA_KROWS))
        r = b * NA_QROWS + np.arange(NA_QROWS)[:, None]
        kr = ks + np.arange(NA_KROWS)[None, :]
        rs = np.clip(r - NA_WIN_ROWS // 2, 0, rows - NA_WIN_ROWS)
        ok = (kr >= rs) & (kr < rs + NA_WIN_ROWS)
        didx = np.where(ok, kr - r + NA_WIN_ROWS - 1, nd)
        blk = tab[:, didx]
        variants.append(blk.transpose(0, 1, 3, 2, 4).reshape(heads, NA_QROWS * w, NA_KROWS * w))
    return jnp.stack(variants)


def _na_kernel(q_ref, *rest, nkb, dh, scale):
    k_refs = rest[:nkb]
    v_refs = rest[nkb:2 * nkb]
    b_ref, o_ref = rest[2 * nkb:]
    kw = k_refs[0].shape[0]
    for hd in range(q_ref.shape[1] // dh):
        cols = slice(hd * dh, (hd + 1) * dh)
        q = q_ref[:, cols]
        s = [lax.dot_general(q, k_refs[j][:, cols], (((1,), (1,)), ((), ())), preferred_element_type=F32) * scale
             + b_ref[hd, :, j * kw:(j + 1) * kw] for j in range(nkb)]
        m = functools.reduce(jnp.maximum, [jnp.max(x, axis=-1, keepdims=True) for x in s])
        p = [jnp.exp(x - m) for x in s]
        inv = 1.0 / functools.reduce(jnp.add, [jnp.sum(x, axis=-1, keepdims=True) for x in p])
        acc = None
        for j in range(nkb):
            part = jnp.dot((p[j] * inv).astype(BF16), v_refs[j][:, cols], preferred_element_type=F32)
            acc = part if acc is None else acc + part
        o_ref[:, cols] = acc.astype(o_ref.dtype)


def _neighbourhood_attention(qkv, bias):
    t = qkv.shape[0]
    heads = NA_HEADS
    dh = qkv.shape[1] // (3 * heads)
    rows = t // GRID_W
    nb = rows // NA_QROWS
    tq = NA_QROWS * GRID_W
    kw = tq // 2
    nkb = NA_KROWS * GRID_W // kw
    max_kb = (rows - NA_KROWS) * GRID_W // kw

    def kstart(b):
        return jnp.clip(2 * b - 1, 0, max_kb)

    def variant(b):
        return jnp.where(b == 0, 0, jnp.where(b == nb - 1, 2, 1))

    hps = NA_HEADS_PER_STEP
    hg = heads // hps
    kspecs = [pl.BlockSpec((kw, hps * dh), functools.partial(lambda h, b, j: (kstart(b) + j, hg + h), j=j))
              for j in range(nkb)]
    vspecs = [pl.BlockSpec((kw, hps * dh), functools.partial(lambda h, b, j: (kstart(b) + j, 2 * hg + h), j=j))
              for j in range(nkb)]
    return pl.pallas_call(
        functools.partial(_na_kernel, nkb=nkb, dh=dh, scale=dh ** -0.5),
        grid=(hg, nb),
        in_specs=[pl.BlockSpec((tq, hps * dh), lambda h, b: (b, h))] + kspecs + vspecs
        + [pl.BlockSpec((None, hps, tq, nkb * kw), lambda h, b: (variant(b), h, 0, 0))],
        out_specs=pl.BlockSpec((tq, hps * dh), lambda h, b: (b, h)),
        out_shape=jax.ShapeDtypeStruct((t, heads * dh), BF16),
        compiler_params=_params(("parallel", "arbitrary")),
        name="neighbourhood_attention",
    )(qkv, *([qkv] * (2 * nkb)), bias)


def _norm_router_kernel(x_ref, g_ref, wr_ref, hn_ref, route_ref, *, n_experts):
    hn = _rms(x_ref[...], g_ref[...])
    hn_ref[...] = hn
    logits = jnp.dot(hn, wr_ref[...], precision=lax.Precision.HIGHEST, preferred_element_type=F32)
    lane = lax.broadcasted_iota(jnp.int32, logits.shape, 1)
    lg = jnp.where(lane < n_experts, logits, -jnp.inf)
    m1 = jnp.max(lg, axis=-1, keepdims=True)
    i1 = jnp.min(jnp.where(lg == m1, lane, LANES), axis=-1, keepdims=True)
    lg2 = jnp.where(lane == i1, -jnp.inf, lg)
    m2 = jnp.max(lg2, axis=-1, keepdims=True)
    i2 = jnp.min(jnp.where(lg2 == m2, lane, LANES), axis=-1, keepdims=True)
    e = jnp.exp(m2 - m1)
    g1 = 1.0 / (1.0 + e)
    g2 = e / (1.0 + e)
    route_ref[...] = jnp.where(lane == 0, i1.astype(F32), jnp.where(lane == 1, i2.astype(F32),
                               jnp.where(lane == 2, g1, jnp.where(lane == 3, g2, 0.0))))


def _norm_router(h, gain, w_router, tm=512):
    t, d = h.shape
    tm = min(tm, t)
    n_experts = w_router.shape[1]
    wr = jnp.zeros((d, LANES), F32).at[:, :n_experts].set(w_router)
    return pl.pallas_call(
        functools.partial(_norm_router_kernel, n_experts=n_experts),
        grid=(t // tm,),
        in_specs=[pl.BlockSpec((tm, d), lambda i: (i, 0)), pl.BlockSpec((1, d), lambda i: (0, 0)),
                  pl.BlockSpec((d, LANES), lambda i: (0, 0))],
        out_specs=[pl.BlockSpec((tm, d), lambda i: (i, 0)), pl.BlockSpec((tm, LANES), lambda i: (i, 0))],
        out_shape=[jax.ShapeDtypeStruct((t, d), F32), jax.ShapeDtypeStruct((t, LANES), F32)],
        compiler_params=_params(("parallel",)),
        name="norm_router",
    )(h, gain.reshape(1, d), wr)


def _row_copy(src_hbm, dst_vmem, sem, src_row, dst_row):
    return pltpu.make_async_copy(src_hbm.at[pl.ds(src_row, 1)], dst_vmem.at[pl.ds(dst_row, 1)], sem)


def _gather_kernel(tok_ref, x_hbm, o_ref, sem):
    rows = o_ref.shape[0]

    def issue(i, carry):
        _row_copy(x_hbm, o_ref, sem, tok_ref[0, 0, i], i).start()
        return carry

    lax.fori_loop(0, rows, issue, 0)
    pltpu.make_async_copy(x_hbm.at[pl.ds(0, rows)], o_ref, sem).wait()


def _gather_rows(x, tok_sorted):
    n_pad = tok_sorted.shape[0]
    d = x.shape[1]
    r = GATHER_ROWS
    return pl.pallas_call(
        _gather_kernel,
        grid=(n_pad // r,),
        in_specs=[pl.BlockSpec((1, 1, r), lambda i: (i, 0, 0), memory_space=pltpu.SMEM),
                  pl.BlockSpec(memory_space=pl.ANY)],
        out_specs=pl.BlockSpec((r, d), lambda i: (i, 0)),
        out_shape=jax.ShapeDtypeStruct((n_pad, d), F32),
        scratch_shapes=[pltpu.SemaphoreType.DMA(())],
        compiler_params=_params(("arbitrary",)),
        name="moe_gather",
    )(tok_sorted.reshape(n_pad // r, 1, r), x)


def _expert_kernel(be_ref, nv_ref, x_ref, wg_ref, wu_ref, wd_ref, o_ref, xb_ref):
    m = pl.program_id(0)
    f = pl.program_id(1)

    @pl.when(f == 0)
    def _():
        xb_ref[...] = x_ref[...].astype(BF16)
        o_ref[...] = jnp.zeros_like(o_ref)

    @pl.when(m < nv_ref[0])
    def _():
        x = xb_ref[...]
        g = jnp.dot(x, wg_ref[...], preferred_element_type=F32)
        u = jnp.dot(x, wu_ref[...], preferred_element_type=F32)
        o_ref[...] += jnp.dot((_silu(g) * u).astype(BF16), wd_ref[...], preferred_element_type=F32)


def _experts(xs, block_e, n_valid, w_gate, w_up, w_down, tf=1024):
    n_pad, d = xs.shape
    fdim = w_gate.shape[2]
    tm = MOE_ROWS
    nb = n_pad // tm
    nf = fdim // tf

    def fidx(m, f, nv):
        return jnp.where(m < nv[0], f, nf - 1)

    grid_spec = pltpu.PrefetchScalarGridSpec(
        num_scalar_prefetch=2,
        grid=(nb, nf),
        in_specs=[
            pl.BlockSpec((tm, d), lambda m, f, be, nv: (jnp.minimum(m, nv[0] - 1), 0)),
            pl.BlockSpec((None, d, tf), lambda m, f, be, nv: (be[m], 0, fidx(m, f, nv))),
            pl.BlockSpec((None, d, tf), lambda m, f, be, nv: (be[m], 0, fidx(m, f, nv))),
            pl.BlockSpec((None, tf, d), lambda m, f, be, nv: (be[m], fidx(m, f, nv), 0)),
        ],
        out_specs=pl.BlockSpec((tm, d), lambda m, f, be, nv: (m, 0)),
        scratch_shapes=[pltpu.VMEM((tm, d), BF16)],
    )
    return pl.pallas_call(
        _expert_kernel,
        grid_spec=grid_spec,
        out_shape=jax.ShapeDtypeStruct((n_pad, d), F32),
        compiler_params=_params(("arbitrary", "arbitrary")),
        name="moe_experts",
    )(block_e, n_valid, xs, w_gate, w_up, w_down)


def _combine_kernel(pos_ref, h_ref, route_ref, gain_ref, y_hbm, o_ref, y0_ref, y1_ref, sem, *, final_norm):
    rows = h_ref.shape[0]

    def issue(i, carry):
        _row_copy(y_hbm, y0_ref, sem.at[0], pos_ref[0, 0, 2 * i], i).start()
        _row_copy(y_hbm, y1_ref, sem.at[1], pos_ref[0, 0, 2 * i + 1], i).start()
        return carry

    lax.fori_loop(0, rows, issue, 0)
    pltpu.make_async_copy(y_hbm.at[pl.ds(0, rows)], y0_ref, sem.at[0]).wait()
    pltpu.make_async_copy(y_hbm.at[pl.ds(0, rows)], y1_ref, sem.at[1]).wait()
    g1 = route_ref[:, 2:3]
    g2 = route_ref[:, 3:4]
    hh = h_ref[...] + (g1 * y0_ref[...] + g2 * y1_ref[...])
    o_ref[...] = _rms(hh, gain_ref[...]) if final_norm else hh


def _combine(h, y, pos, route, final_gain, final_norm):
    t, d = h.shape
    r = GATHER_ROWS
    return pl.pallas_call(
        functools.partial(_combine_kernel, final_norm=final_norm),
        grid=(t // r,),
        in_specs=[pl.BlockSpec((1, 1, MOE_TOP_K * r), lambda i: (i, 0, 0), memory_space=pltpu.SMEM),
                  pl.BlockSpec((r, d), lambda i: (i, 0)),
                  pl.BlockSpec((r, LANES), lambda i: (i, 0)),
                  pl.BlockSpec((1, d), lambda i: (0, 0)),
                  pl.BlockSpec(memory_space=pl.ANY)],
        out_specs=pl.BlockSpec((r, d), lambda i: (i, 0)),
        out_shape=jax.ShapeDtypeStruct((t, d), F32),
        scratch_shapes=[pltpu.VMEM((r, d), F32), pltpu.VMEM((r, d), F32), pltpu.SemaphoreType.DMA((2,))],
        compiler_params=_params(("arbitrary",)),
        name="moe_combine_norm",
    )(pos.reshape(t // r, 1, MOE_TOP_K * r), h, route, final_gain.reshape(1, d), y)


def _routing_plan(route, n_experts):
    t = route.shape[0]
    n_assign = t * MOE_TOP_K
    e_flat = route[:, :MOE_TOP_K].astype(jnp.int32).reshape(-1)
    onehot = (e_flat[:, None] == jnp.arange(n_experts, dtype=jnp.int32)[None, :]).astype(jnp.int32)
    csum = jnp.cumsum(onehot, axis=0)
    rank = jnp.take_along_axis(csum, e_flat[:, None], axis=1)[:, 0] - 1
    counts = csum[-1]
    padded = (counts + MOE_ROWS - 1) // MOE_ROWS * MOE_ROWS
    pend = jnp.cumsum(padded)
    pos = (pend - padded)[e_flat] + rank
    n_blocks = n_assign // MOE_ROWS + n_experts
    tok = jnp.arange(n_assign, dtype=jnp.int32) // MOE_TOP_K
    tok_sorted = jnp.zeros((n_blocks * MOE_ROWS,), jnp.int32).at[pos].set(tok)
    n_valid = (pend[-1] // MOE_ROWS).astype(jnp.int32)
    blk = jnp.arange(n_blocks, dtype=jnp.int32)
    block_e = jnp.sum((blk * MOE_ROWS)[:, None] >= pend[None, :], axis=1).astype(jnp.int32)
    block_e = jnp.where(blk < n_valid, block_e, block_e[n_valid - 1])
    return pos.astype(jnp.int32), tok_sorted, block_e, n_valid.reshape(1)


def kernel(x, mem, mem_norm_gain, mem_w_kv, norm_gain, ret_w_in, ret_w_out, ret_decay_logit, ret_gn_gain, ffn_w_gate, ffn_w_up, ffn_w_down, na_w_in, na_w_out, na_rpb, moe_router, moe_w_gate, moe_w_up, moe_w_down, xa_w_q, xa_w_out, final_norm_gain):
    b, s, d = x.shape
    assert b == 1
    depth = norm_gain.shape[0]
    bf = lambda w: w.astype(BF16)

    mem_kv = _matmul(_rmsnorm(mem[0], mem_norm_gain, BF16), bf(mem_w_kv), BF16, name="mem_kv")
    dk = (ret_w_in.shape[2] - 2 * ret_gn_gain.shape[1]) // (2 * RET_HEADS)
    cos, sin = _rotary_tables(s, dk)

    h = x[0]
    for i in range(depth):
        j = i // 2
        hn = _rmsnorm(h, norm_gain[i, 0], BF16)
        if i % 2 == 0:
            w_in = bf(ret_w_in[j])
            n_qk = 2 * RET_HEADS * dk
            qk = _matmul(hn, w_in, BF16, name="ret_in_qk", cols=(0, n_qk), rotary=(cos, sin, dk, n_qk // 2))
            vg = _matmul(hn, w_in, BF16, name="ret_in_vg", cols=(n_qk, w_in.shape[1] - n_qk))
            log_decay = jax.nn.log_sigmoid(ret_decay_logit[j].astype(F32))
            yg = _retention(qk, vg, log_decay, ret_gn_gain[j])
            h = _matmul(yg, bf(ret_w_out[j]), F32, residual=h, name="ret_out")
        else:
            qkv = _matmul(hn, bf(na_w_in[j]), BF16, name="na_in")
            o = _neighbourhood_attention(qkv, _na_bias_tables(na_rpb[j], s // GRID_W))
            h = _matmul(o, bf(na_w_out[j]), F32, residual=h, name="na_out")
        h = _xattn(h, norm_gain[i, 1], mem_kv, bf(xa_w_q[i]), bf(xa_w_out[i]))
        if i % 2 == 0:
            hn = _rmsnorm(h, norm_gain[i, 2], BF16)
            h = _ffn(hn, bf(ffn_w_gate[j]), bf(ffn_w_up[j]), bf(ffn_w_down[j]), h)
        else:
            hn32, route = _norm_router(h, norm_gain[i, 2], moe_router[j])
            pos, tok_sorted, block_e, n_valid = _routing_plan(route, moe_router.shape[2])
            xs = _gather_rows(hn32, tok_sorted)
            y = _experts(xs, block_e, n_valid, bf(moe_w_gate[j]), bf(moe_w_up[j]), bf(moe_w_down[j]))
            h = _combine(h, y, pos, route, final_norm_gain, final_norm=(i == depth - 1))
    if depth % 2 == 1:
        h = _rmsnorm(h, final_norm_gain, F32)
    return h[None]
```

```python
import functools

import numpy as np
import jax
import jax.numpy as jnp
from jax import lax
from jax.experimental import pallas as pl
from jax.experimental.pallas import tpu as pltpu

F32 = jnp.float32
BF16 = jnp.bfloat16

GRID_W = 64
RMS_EPS = 1e-6
GN_EPS = 1e-6
RET_HEADS = 8
ROPE_BASE = 10000.0
NA_HEADS = 16
NA_WIN_ROWS = 8
NA_WIN_COLS = 16
XA_HEADS = 4
XA_HEAD_DIM = 128
MOE_TOP_K = 2

V7X_VMEM_BYTES = 64 * 1024 * 1024
VMEM_LIMIT = V7X_VMEM_BYTES * 3 // 4
LANES = 128

RET_CHUNK = 256
RET_STEP_ROWS = 2048
NA_QROWS = 8
NA_KROWS = 2 * NA_QROWS
NA_HEADS_PER_STEP = 4
MOE_ROWS = 512
GATHER_ROWS = 256
NEG = -1e30


def _params(sem):
    return pltpu.CompilerParams(dimension_semantics=sem, vmem_limit_bytes=VMEM_LIMIT)


def _silu(g):
    return g / (1.0 + jnp.exp(-g))


def _rms(x, gain):
    return x * lax.rsqrt(jnp.mean(x * x, axis=-1, keepdims=True) + RMS_EPS) * gain


def _rmsnorm_kernel(x_ref, g_ref, o_ref):
    o_ref[...] = _rms(x_ref[...], g_ref[...]).astype(o_ref.dtype)


def _rmsnorm(x, gain, out_dtype, tm=512):
    m, d = x.shape
    tm = min(tm, m)
    return pl.pallas_call(
        _rmsnorm_kernel,
        grid=(m // tm,),
        in_specs=[pl.BlockSpec((tm, d), lambda i: (i, 0)), pl.BlockSpec((1, d), lambda i: (0, 0))],
        out_specs=pl.BlockSpec((tm, d), lambda i: (i, 0)),
        out_shape=jax.ShapeDtypeStruct((m, d), out_dtype),
        compiler_params=_params(("parallel",)),
        name="rmsnorm",
    )(x, gain.reshape(1, d))


def _mm_kernel(a_ref, w_ref, o_ref):
    o_ref[...] = jnp.dot(a_ref[...], w_ref[...], preferred_element_type=F32).astype(o_ref.dtype)


def _mm_res_kernel(a_ref, w_ref, r_ref, o_ref):
    o_ref[...] = (r_ref[...] + jnp.dot(a_ref[...], w_ref[...], preferred_element_type=F32)).astype(o_ref.dtype)


def _mm_rotary_kernel(a_ref, w_ref, cos_ref, sin_ref, o_ref, *, dk, first_k_tile):
    acc = jnp.dot(a_ref[...], w_ref[...], preferred_element_type=F32)
    cos = cos_ref[...]
    sin = sin_ref[...]
    half = dk // 2
    scale = jnp.where(pl.program_id(1) >= first_k_tile, dk ** -0.5, 1.0)
    for lo in range(0, acc.shape[1], dk):
        x1 = acc[:, lo:lo + half]
        x2 = acc[:, lo + half:lo + dk]
        o_ref[:, lo:lo + half] = ((x1 * cos - x2 * sin) * scale).astype(o_ref.dtype)
        o_ref[:, lo + half:lo + dk] = ((x2 * cos + x1 * sin) * scale).astype(o_ref.dtype)


def _matmul(a, w, out_dtype, residual=None, name="matmul", cols=None, rotary=None):
    m, k = a.shape
    c0, n = cols if cols is not None else (0, w.shape[1])
    tm = min(m, 1024 if k <= 2048 else 512)
    tn = min(n, 1024)
    j0 = c0 // tn
    in_specs = [pl.BlockSpec((tm, k), lambda i, j: (i, 0)), pl.BlockSpec((k, tn), lambda i, j: (0, j0 + j))]
    args = [a, w]
    body = _mm_kernel
    if residual is not None:
        in_specs.append(pl.BlockSpec((tm, tn), lambda i, j: (i, j)))
        args.append(residual)
        body = _mm_res_kernel
    if rotary is not None:
        cos, sin, dk, n_q_cols = rotary
        in_specs += [pl.BlockSpec((tm, dk // 2), lambda i, j: (i, 0))] * 2
        args += [cos, sin]
        body = functools.partial(_mm_rotary_kernel, dk=dk, first_k_tile=n_q_cols // tn)
    return pl.pallas_call(
        body,
        grid=(m // tm, n // tn),
        in_specs=in_specs,
        out_specs=pl.BlockSpec((tm, tn), lambda i, j: (i, j)),
        out_shape=jax.ShapeDtypeStruct((m, n), out_dtype),
        compiler_params=_params(("parallel", "parallel")),
        name=name,
    )(*args)


def _ret_kernel(ld_ref, q_ref, k_ref, v_ref, *rest, backward, n):
    if backward:
        yf_ref, g_ref, gain_ref, o_ref, state_ref = rest
    else:
        o_ref, state_ref = rest
    h = pl.program_id(0)

    @pl.when(pl.program_id(1) == 0)
    def _():
        state_ref[...] = jnp.zeros_like(state_ref)

    ld = ld_ref[h]
    n_sub = q_ref.shape[0] // n

    i = lax.broadcasted_iota(jnp.int32, (n, n), 0)
    j = lax.broadcasted_iota(jnp.int32, (n, n), 1)
    pos = lax.broadcasted_iota(jnp.int32, (n, 1), 0).astype(F32)
    if backward:
        rel = (j - i).astype(F32)
        keep = j > i
        xi = jnp.exp(ld * (n - pos))
        zeta = jnp.exp(ld * pos)
    else:
        rel = (i - j).astype(F32)
        keep = i >= j
        xi = jnp.exp(ld * (pos + 1.0))
        zeta = jnp.exp(ld * (n - 1.0 - pos))
    decay = jnp.where(keep, jnp.exp(ld * jnp.maximum(rel, 0.0)), 0.0)
    chunk_decay = jnp.exp(ld * n)

    for s in range(n_sub):
        rows = slice((n_sub - 1 - s) * n, (n_sub - s) * n) if backward else slice(s * n, (s + 1) * n)
        q = q_ref[rows, :]
        k = k_ref[rows, :]
        v = v_ref[rows, :]
        scores = lax.dot_general(q, k, (((1,), (1,)), ((), ())), preferred_element_type=F32) * decay
        inner = jnp.dot(scores.astype(BF16), v, preferred_element_type=F32)
        state = state_ref[...]
        cross = jnp.dot((q.astype(F32) * xi).astype(BF16), state.astype(BF16), preferred_element_type=F32)
        state_ref[...] = chunk_decay * state + lax.dot_general(
            (k.astype(F32) * zeta).astype(BF16), v, (((0,), (0,)), ((), ())), preferred_element_type=F32)
        y = inner + cross
        if backward:
            y = y + yf_ref[rows, :]
            mu = jnp.mean(y, axis=-1, keepdims=True)
            d = y - mu
            var = jnp.mean(d * d, axis=-1, keepdims=True)
            yn = d * lax.rsqrt(var + GN_EPS) * gain_ref[...]
            o_ref[rows, :] = (_silu(g_ref[rows, :].astype(F32)) * yn).astype(o_ref.dtype)
        else:
            o_ref[rows, :] = y


def _retention(qk, vg, log_decay, gn_gain):
    t = qk.shape[0]
    hh = RET_HEADS
    dk = qk.shape[1] // (2 * hh)
    dv = vg.shape[1] // (2 * hh)
    c = min(RET_CHUNK, t)
    rows = min(RET_STEP_ROWS, t)
    ns = t // rows

    def specs(order):
        return [
            pl.BlockSpec(memory_space=pltpu.SMEM),
            pl.BlockSpec((rows, dk), lambda h, i: (order(i), h)),
            pl.BlockSpec((rows, dk), lambda h, i: (order(i), hh + h)),
            pl.BlockSpec((rows, dv), lambda h, i: (order(i), h)),
        ]

    fwd = lambda i: i
    bwd = lambda i: ns - 1 - i
    y_fwd = pl.pallas_call(
        functools.partial(_ret_kernel, backward=False, n=c),
        grid=(hh, ns),
        in_specs=specs(fwd),
        out_specs=pl.BlockSpec((rows, dv), lambda h, i: (i, h)),
        out_shape=jax.ShapeDtypeStruct((t, hh * dv), F32),
        scratch_shapes=[pltpu.VMEM((dk, dv), F32)],
        compiler_params=_params(("parallel", "arbitrary")),
        name="retention_fwd",
    )(log_decay[0], qk, qk, vg)
    return pl.pallas_call(
        functools.partial(_ret_kernel, backward=True, n=c),
        grid=(hh, ns),
        in_specs=specs(bwd) + [
            pl.BlockSpec((rows, dv), lambda h, i: (bwd(i), h)),
            pl.BlockSpec((rows, dv), lambda h, i: (bwd(i), hh + h)),
            pl.BlockSpec((1, dv), lambda h, i: (0, h)),
        ],
        out_specs=pl.BlockSpec((rows, dv), lambda h, i: (bwd(i), h)),
        out_shape=jax.ShapeDtypeStruct((t, hh * dv), BF16),
        scratch_shapes=[pltpu.VMEM((dk, dv), F32)],
        compiler_params=_params(("parallel", "arbitrary")),
        name="retention_bwd",
    )(log_decay[1], qk, qk, vg, y_fwd, vg, gn_gain.reshape(1, -1))


def _rotary_tables(t, dk):
    inv_freq = jnp.power(ROPE_BASE, -jnp.arange(0, dk, 2, dtype=F32) / dk)
    ang = jnp.arange(t, dtype=F32)[:, None] * inv_freq[None, :]
    return jnp.cos(ang), jnp.sin(ang)


def _xattn_kernel(h_ref, g_ref, wq_ref, kv_ref, wo_ref, o_ref):
    x = h_ref[...]
    hn = _rms(x, g_ref[...]).astype(BF16)
    q = jnp.dot(hn, wq_ref[...], preferred_element_type=F32)
    inner = XA_HEADS * XA_HEAD_DIM
    outs = []
    for hd in range(XA_HEADS):
        lo = hd * XA_HEAD_DIM
        qh = q[:, lo:lo + XA_HEAD_DIM].astype(BF16)
        kh = kv_ref[:, lo:lo + XA_HEAD_DIM]
        vh = kv_ref[:, inner + lo:inner + lo + XA_HEAD_DIM]
        s = lax.dot_general(qh, kh, (((1,), (1,)), ((), ())), preferred_element_type=F32) * (XA_HEAD_DIM ** -0.5)
        p = jnp.exp(s - jnp.max(s, axis=-1, keepdims=True))
        p = p / jnp.sum(p, axis=-1, keepdims=True)
        outs.append(jnp.dot(p.astype(BF16), vh, preferred_element_type=F32))
    o = jnp.concatenate(outs, axis=1).astype(BF16)
    o_ref[...] = x + jnp.dot(o, wo_ref[...], preferred_element_type=F32)


def _xattn(h, gain, mem_kv, w_q, w_out, tm=512):
    t, d = h.shape
    tm = min(tm, t)
    full = lambda a: pl.BlockSpec(a.shape, lambda i: (0,) * a.ndim)
    gain = gain.reshape(1, d)
    return pl.pallas_call(
        _xattn_kernel,
        grid=(t // tm,),
        in_specs=[pl.BlockSpec((tm, d), lambda i: (i, 0)), full(gain), full(w_q), full(mem_kv), full(w_out)],
        out_specs=pl.BlockSpec((tm, d), lambda i: (i, 0)),
        out_shape=jax.ShapeDtypeStruct((t, d), F32),
        compiler_params=_params(("parallel",)),
        name="mem_xattn",
    )(h, gain, w_q, mem_kv, w_out)


def _ffn_kernel(x_ref, wg_ref, wu_ref, wd_ref, r_ref, o_ref):
    @pl.when(pl.program_id(1) == 0)
    def _():
        o_ref[...] = r_ref[...]

    x = x_ref[...]
    g = jnp.dot(x, wg_ref[...], preferred_element_type=F32)
    u = jnp.dot(x, wu_ref[...], preferred_element_type=F32)
    o_ref[...] += jnp.dot((_silu(g) * u).astype(BF16), wd_ref[...], preferred_element_type=F32)


def _ffn(x, w_gate, w_up, w_down, residual, tm=512, tf=512):
    t, d = x.shape
    f = w_gate.shape[1]
    tm = min(tm, t)
    return pl.pallas_call(
        _ffn_kernel,
        grid=(t // tm, f // tf),
        in_specs=[
            pl.BlockSpec((tm, d), lambda i, j: (i, 0)),
            pl.BlockSpec((d, tf), lambda i, j: (0, j)),
            pl.BlockSpec((d, tf), lambda i, j: (0, j)),
            pl.BlockSpec((tf, d), lambda i, j: (j, 0)),
            pl.BlockSpec((tm, d), lambda i, j: (i, 0)),
        ],
        out_specs=pl.BlockSpec((tm, d), lambda i, j: (i, 0)),
        out_shape=jax.ShapeDtypeStruct((t, d), F32),
        compiler_params=_params(("parallel", "arbitrary")),
        name="swiglu",
    )(x, w_gate, w_up, w_down, residual)


def _na_bias_tables(rpb, rows):
    heads = rpb.shape[0]
    w = GRID_W
    nd = 2 * NA_WIN_ROWS - 1
    ncol = 2 * NA_WIN_COLS - 1
    qc = np.arange(w)[:, None]
    kc = np.arange(w)[None, :]
    cs = np.clip(qc - NA_WIN_COLS // 2, 0, w - NA_WIN_COLS)
    col_ok = (kc >= cs) & (kc < cs + NA_WIN_COLS)
    dc = np.clip(kc - qc + NA_WIN_COLS - 1, 0, ncol - 1)
    sel = (dc.reshape(-1)[None, :] == np.arange(ncol)[:, None]).astype(np.float32)
    tab = jnp.dot(rpb.reshape(heads * nd, ncol), jnp.asarray(sel), precision=lax.Precision.HIGHEST)
    tab = jnp.where(jnp.asarray(col_ok.reshape(-1))[None, :], tab, NEG).reshape(heads, nd, w, w)
    tab = jnp.concatenate([tab, jnp.full((heads, 1, w, w), NEG, F32)], axis=1)
    nb = rows // NA_QROWS
    variants = []
    for b in (0, 1, nb - 1):
        ks = int(np.clip(b * NA_QROWS - NA_WIN_ROWS // 2, 0, rows - NA_KROWS))
        r = b * NA_QROWS + np.arange(NA_QROWS)[:, None]
        kr = ks + np.arange(NA_KROWS)[None, :]
        rs = np.clip(r - NA_WIN_ROWS // 2, 0, rows - NA_WIN_ROWS)
        ok = (kr >= rs) & (kr < rs + NA_WIN_ROWS)
        didx = np.where(ok, kr - r + NA_WIN_ROWS - 1, nd)
        blk = tab[:, didx]
        variants.append(blk.transpose(0, 1, 3, 2, 4).reshape(heads, NA_QROWS * w, NA_KROWS * w))
    return jnp.stack(variants)


def _na_kernel(q_ref, *rest, nkb, dh, scale):
    k_refs = rest[:nkb]
    v_refs = rest[nkb:2 * nkb]
    b_ref, o_ref = rest[2 * nkb:]
    kw = k_refs[0].shape[0]
    for hd in range(q_ref.shape[1] // dh):
        cols = slice(hd * dh, (hd + 1) * dh)
        q = q_ref[:, cols]
        s = [lax.dot_general(q, k_refs[j][:, cols], (((1,), (1,)), ((), ())), preferred_element_type=F32) * scale
             + b_ref[hd, :, j * kw:(j + 1) * kw] for j in range(nkb)]
        m = functools.reduce(jnp.maximum, [jnp.max(x, axis=-1, keepdims=True) for x in s])
        p = [jnp.exp(x - m) for x in s]
        inv = 1.0 / functools.reduce(jnp.add, [jnp.sum(x, axis=-1, keepdims=True) for x in p])
        acc = None
        for j in range(nkb):
            part = jnp.dot((p[j] * inv).astype(BF16), v_refs[j][:, cols], preferred_element_type=F32)
            acc = part if acc is None else acc + part
        o_ref[:, cols] = acc.astype(o_ref.dtype)


def _neighbourhood_attention(qkv, bias):
    t = qkv.shape[0]
    heads = NA_HEADS
    dh = qkv.shape[1] // (3 * heads)
    rows = t // GRID_W
    nb = rows // NA_QROWS
    tq = NA_QROWS * GRID_W
    kw = tq // 2
    nkb = NA_KROWS * GRID_W // kw
    max_kb = (rows - NA_KROWS) * GRID_W // kw

    def kstart(b):
        return jnp.clip(2 * b - 1, 0, max_kb)

    def variant(b):
        return jnp.where(b == 0, 0, jnp.where(b == nb - 1, 2, 1))

    hps = NA_HEADS_PER_STEP
    hg = heads // hps
    kspecs = [pl.BlockSpec((kw, hps * dh), functools.partial(lambda h, b, j: (kstart(b) + j, hg + h), j=j))
              for j in range(nkb)]
    vspecs = [pl.BlockSpec((kw, hps * dh), functools.partial(lambda h, b, j: (kstart(b) + j, 2 * hg + h), j=j))
              for j in range(nkb)]
    return pl.pallas_call(
        functools.partial(_na_kernel, nkb=nkb, dh=dh, scale=dh ** -0.5),
        grid=(hg, nb),
        in_specs=[pl.BlockSpec((tq, hps * dh), lambda h, b: (b, h))] + kspecs + vspecs
        + [pl.BlockSpec((None, hps, tq, nkb * kw), lambda h, b: (variant(b), h, 0, 0))],
        out_specs=pl.BlockSpec((tq, hps * dh), lambda h, b: (b, h)),
        out_shape=jax.ShapeDtypeStruct((t, heads * dh), BF16),
        compiler_params=_params(("parallel", "arbitrary")),
        name="neighbourhood_attention",
    )(qkv, *([qkv] * (2 * nkb)), bias)


def _norm_router_kernel(x_ref, g_ref, wr_ref, hn_ref, route_ref, *, n_experts):
    hn = _rms(x_ref[...], g_ref[...])
    hn_ref[...] = hn
    logits = jnp.dot(hn, wr_ref[...], precision=lax.Precision.HIGHEST, preferred_element_type=F32)
    lane = lax.broadcasted_iota(jnp.int32, logits.shape, 1)
    lg = jnp.where(lane < n_experts, logits, -jnp.inf)
    m1 = jnp.max(lg, axis=-1, keepdims=True)
    i1 = jnp.min(jnp.where(lg == m1, lane, LANES), axis=-1, keepdims=True)
    lg2 = jnp.where(lane == i1, -jnp.inf, lg)
    m2 = jnp.max(lg2, axis=-1, keepdims=True)
    i2 = jnp.min(jnp.where(lg2 == m2, lane, LANES), axis=-1, keepdims=True)
    e = jnp.exp(m2 - m1)
    g1 = 1.0 / (1.0 + e)
    g2 = e / (1.0 + e)
    route_ref[...] = jnp.where(lane == 0, i1.astype(F32), jnp.where(lane == 1, i2.astype(F32),
                               jnp.where(lane == 2, g1, jnp.where(lane == 3, g2, 0.0))))


def _norm_router(h, gain, w_router, tm=512):
    t, d = h.shape
    tm = min(tm, t)
    n_experts = w_router.shape[1]
    wr = jnp.zeros((d, LANES), F32).at[:, :n_experts].set(w_router)
    return pl.pallas_call(
        functools.partial(_norm_router_kernel, n_experts=n_experts),
        grid=(t // tm,),
        in_specs=[pl.BlockSpec((tm, d), lambda i: (i, 0)), pl.BlockSpec((1, d), lambda i: (0, 0)),
                  pl.BlockSpec((d, LANES), lambda i: (0, 0))],
        out_specs=[pl.BlockSpec((tm, d), lambda i: (i, 0)), pl.BlockSpec((tm, LANES), lambda i: (i, 0))],
        out_shape=[jax.ShapeDtypeStruct((t, d), F32), jax.ShapeDtypeStruct((t, LANES), F32)],
        compiler_params=_params(("parallel",)),
        name="norm_router",
    )(h, gain.reshape(1, d), wr)


def _row_copy(src_hbm, dst_vmem, sem, src_row, dst_row):
    return pltpu.make_async_copy(src_hbm.at[pl.ds(src_row, 1)], dst_vmem.at[pl.ds(dst_row, 1)], sem)


def _gather_kernel(tok_ref, x_hbm, o_ref, sem):
    rows = o_ref.shape[0]

    def issue(i, carry):
        _row_copy(x_hbm, o_ref, sem, tok_ref[0, 0, 2 * i], 2 * i).start(priority=0)
        _row_copy(x_hbm, o_ref, sem, tok_ref[0, 0, 2 * i + 1], 2 * i + 1).start(priority=1)
        return carry

    lax.fori_loop(0, rows // 2, issue, 0)
    pltpu.make_async_copy(x_hbm.at[pl.ds(0, rows)], o_ref, sem).wait()


def _gather_rows(x, tok_sorted):
    n_pad = tok_sorted.shape[0]
    d = x.shape[1]
    r = GATHER_ROWS
    return pl.pallas_call(
        _gather_kernel,
        grid=(n_pad // r,),
        in_specs=[pl.BlockSpec((1, 1, r), lambda i: (i, 0, 0), memory_space=pltpu.SMEM),
                  pl.BlockSpec(memory_space=pl.ANY)],
        out_specs=pl.BlockSpec((r, d), lambda i: (i, 0)),
        out_shape=jax.ShapeDtypeStruct((n_pad, d), F32),
        scratch_shapes=[pltpu.SemaphoreType.DMA(())],
        compiler_params=_params(("arbitrary",)),
        name="moe_gather",
    )(tok_sorted.reshape(n_pad // r, 1, r), x)


def _expert_kernel(be_ref, nv_ref, x_ref, wg_ref, wu_ref, wd_ref, o_ref, xb_ref):
    m = pl.program_id(0)
    f = pl.program_id(1)

    @pl.when(f == 0)
    def _():
        xb_ref[...] = x_ref[...].astype(BF16)
        o_ref[...] = jnp.zeros_like(o_ref)

    @pl.when(m < nv_ref[0])
    def _():
        x = xb_ref[...]
        g = jnp.dot(x, wg_ref[...], preferred_element_type=F32)
        u = jnp.dot(x, wu_ref[...], preferred_element_type=F32)
        o_ref[...] += jnp.dot((_silu(g) * u).astype(BF16), wd_ref[...], preferred_element_type=F32)


def _experts(xs, block_e, n_valid, w_gate, w_up, w_down, tf=1024):
    n_pad, d = xs.shape
    fdim = w_gate.shape[2]
    tm = MOE_ROWS
    nb = n_pad // tm
    nf = fdim // tf

    def fidx(m, f, nv):
        return jnp.where(m < nv[0], f, nf - 1)

    grid_spec = pltpu.PrefetchScalarGridSpec(
        num_scalar_prefetch=2,
        grid=(nb, nf),
        in_specs=[
            pl.BlockSpec((tm, d), lambda m, f, be, nv: (jnp.minimum(m, nv[0] - 1), 0)),
            pl.BlockSpec((None, d, tf), lambda m, f, be, nv: (be[m], 0, fidx(m, f, nv))),
            pl.BlockSpec((None, d, tf), lambda m, f, be, nv: (be[m], 0, fidx(m, f, nv))),
            pl.BlockSpec((None, tf, d), lambda m, f, be, nv: (be[m], fidx(m, f, nv), 0)),
        ],
        out_specs=pl.BlockSpec((tm, d), lambda m, f, be, nv: (m, 0)),
        scratch_shapes=[pltpu.VMEM((tm, d), BF16)],
    )
    return pl.pallas_call(
        _expert_kernel,
        grid_spec=grid_spec,
        out_shape=jax.ShapeDtypeStruct((n_pad, d), F32),
        compiler_params=_params(("arbitrary", "arbitrary")),
        name="moe_experts",
    )(block_e, n_valid, xs, w_gate, w_up, w_down)


def _combine_kernel(pos_ref, h_ref, route_ref, gain_ref, y_hbm, o_ref, y0_ref, y1_ref, sem, *, final_norm):
    rows = h_ref.shape[0]

    def issue(i, carry):
        _row_copy(y_hbm, y0_ref, sem.at[0], pos_ref[0, 0, 2 * i], i).start(priority=0)
        _row_copy(y_hbm, y1_ref, sem.at[1], pos_ref[0, 0, 2 * i + 1], i).start(priority=1)
        return carry

    lax.fori_loop(0, rows, issue, 0)
    pltpu.make_async_copy(y_hbm.at[pl.ds(0, rows)], y0_ref, sem.at[0]).wait()
    pltpu.make_async_copy(y_hbm.at[pl.ds(0, rows)], y1_ref, sem.at[1]).wait()
    g1 = route_ref[:, 2:3]
    g2 = route_ref[:, 3:4]
    hh = h_ref[...] + (g1 * y0_ref[...] + g2 * y1_ref[...])
    o_ref[...] = _rms(hh, gain_ref[...]) if final_norm else hh


def _combine(h, y, pos, route, final_gain, final_norm):
    t, d = h.shape
    r = GATHER_ROWS
    return pl.pallas_call(
        functools.partial(_combine_kernel, final_norm=final_norm),
        grid=(t // r,),
        in_specs=[pl.BlockSpec((1, 1, MOE_TOP_K * r), lambda i: (i, 0, 0), memory_space=pltpu.SMEM),
                  pl.BlockSpec((r, d), lambda i: (i, 0)),
                  pl.BlockSpec((r, LANES), lambda i: (i, 0)),
                  pl.BlockSpec((1, d), lambda i: (0, 0)),
                  pl.BlockSpec(memory_space=pl.ANY)],
        out_specs=pl.BlockSpec((r, d), lambda i: (i, 0)),
        out_shape=jax.ShapeDtypeStruct((t, d), F32),
        scratch_shapes=[pltpu.VMEM((r, d), F32), pltpu.VMEM((r, d), F32), pltpu.SemaphoreType.DMA((2,))],
        compiler_params=_params(("arbitrary",)),
        name="moe_combine_norm",
    )(pos.reshape(t // r, 1, MOE_TOP_K * r), h, route, final_gain.reshape(1, d), y)


def _routing_plan(route, n_experts):
    t = route.shape[0]
    n_assign = t * MOE_TOP_K
    e_flat = route[:, :MOE_TOP_K].astype(jnp.int32).reshape(-1)
    onehot = (e_flat[:, None] == jnp.arange(n_experts, dtype=jnp.int32)[None, :]).astype(jnp.int32)
    csum = jnp.cumsum(onehot, axis=0)
    rank = jnp.take_along_axis(csum, e_flat[:, None], axis=1)[:, 0] - 1
    counts = csum[-1]
    padded = (counts + MOE_ROWS - 1) // MOE_ROWS * MOE_ROWS
    pend = jnp.cumsum(padded)
    pos = (pend - padded)[e_flat] + rank
    n_blocks = n_assign // MOE_ROWS + n_experts
    tok = jnp.arange(n_assign, dtype=jnp.int32) // MOE_TOP_K
    tok_sorted = jnp.zeros((n_blocks * MOE_ROWS,), jnp.int32).at[pos].set(tok)
    n_valid = (pend[-1] // MOE_ROWS).astype(jnp.int32)
    blk = jnp.arange(n_blocks, dtype=jnp.int32)
    block_e = jnp.sum((blk * MOE_ROWS)[:, None] >= pend[None, :], axis=1).astype(jnp.int32)
    block_e = jnp.where(blk < n_valid, block_e, block_e[n_valid - 1])
    return pos.astype(jnp.int32), tok_sorted, block_e, n_valid.reshape(1)


def kernel(x, mem, mem_norm_gain, mem_w_kv, norm_gain, ret_w_in, ret_w_out, ret_decay_logit, ret_gn_gain, ffn_w_gate, ffn_w_up, ffn_w_down, na_w_in, na_w_out, na_rpb, moe_router, moe_w_gate, moe_w_up, moe_w_down, xa_w_q, xa_w_out, final_norm_gain):
    b, s, d = x.shape
    assert b == 1
    depth = norm_gain.shape[0]
    bf = lambda w: w.astype(BF16)

    mem_kv = _matmul(_rmsnorm(mem[0], mem_norm_gain, BF16), bf(mem_w_kv), BF16, name="mem_kv")
    dk = (ret_w_in.shape[2] - 2 * ret_gn_gain.shape[1]) // (2 * RET_HEADS)
    cos, sin = _rotary_tables(s, dk)

    h = x[0]
    for i in range(depth):
        j = i // 2
        hn = _rmsnorm(h, norm_gain[i, 0], BF16)
        if i % 2 == 0:
            w_in = bf(ret_w_in[j])
            n_qk = 2 * RET_HEADS * dk
            qk = _matmul(hn, w_in, BF16, name="ret_in_qk", cols=(0, n_qk), rotary=(cos, sin, dk, n_qk // 2))
            vg = _matmul(hn, w_in, BF16, name="ret_in_vg", cols=(n_qk, w_in.shape[1] - n_qk))
            log_decay = jax.nn.log_sigmoid(ret_decay_logit[j].astype(F32))
            yg = _retention(qk, vg, log_decay, ret_gn_gain[j])
            h = _matmul(yg, bf(ret_w_out[j]), F32, residual=h, name="ret_out")
        else:
            qkv = _matmul(hn, bf(na_w_in[j]), BF16, name="na_in")
            o = _neighbourhood_attention(qkv, _na_bias_tables(na_rpb[j], s // GRID_W))
            h = _matmul(o, bf(na_w_out[j]), F32, residual=h, name="na_out")
        h = _xattn(h, norm_gain[i, 1], mem_kv, bf(xa_w_q[i]), bf(xa_w_out[i]))
        if i % 2 == 0:
            hn = _rmsnorm(h, norm_gain[i, 2], BF16)
            h = _ffn(hn, bf(ffn_w_gate[j]), bf(ffn_w_up[j]), bf(ffn_w_down[j]), h)
        else:
            hn32, route = _norm_router(h, norm_gain[i, 2], moe_router[j])
            pos, tok_sorted, block_e, n_valid = _routing_plan(route, moe_router.shape[2])
            xs = _gather_rows(hn32, tok_sorted)
            y = _experts(xs, block_e, n_valid, bf(moe_w_gate[j]), bf(moe_w_up[j]), bf(moe_w_down[j]))
            h = _combine(h, y, pos, route, final_norm_gain, final_norm=(i == depth - 1))
    if depth % 2 == 1:
        h = _rmsnorm(h, final_norm_gain, F32)
    return h[None]
```

```python
import functools

import numpy as np
import jax
import jax.numpy as jnp
from jax import lax
from jax.experimental import pallas as pl
from jax.experimental.pallas import tpu as pltpu

F32 = jnp.float32
BF16 = jnp.bfloat16

GRID_W = 64
RMS_EPS = 1e-6
GN_EPS = 1e-6
RET_HEADS = 8
ROPE_BASE = 10000.0
NA_HEADS = 16
NA_WIN_ROWS = 8
NA_WIN_COLS = 16
XA_HEADS = 4
XA_HEAD_DIM = 128
MOE_TOP_K = 2

V7X_VMEM_BYTES = 64 * 1024 * 1024
VMEM_LIMIT = V7X_VMEM_BYTES * 3 // 4
LANES = 128

RET_CHUNK = 256
RET_STEP_ROWS = 2048
NA_QROWS = 8
NA_KROWS = 2 * NA_QROWS
NA_HEADS_PER_STEP = 4
MOE_ROWS = 512
GATHER_ROWS = 256
NEG = -1e30


def _params(sem):
    return pltpu.CompilerParams(dimension_semantics=sem, vmem_limit_bytes=VMEM_LIMIT)


def _silu(g):
    return g / (1.0 + jnp.exp(-g))


def _rms(x, gain):
    return x * lax.rsqrt(jnp.mean(x * x, axis=-1, keepdims=True) + RMS_EPS) * gain


def _rmsnorm_kernel(x_ref, g_ref, o_ref):
    o_ref[...] = _rms(x_ref[...], g_ref[...]).astype(o_ref.dtype)


def _rmsnorm(x, gain, out_dtype, tm=512):
    m, d = x.shape
    tm = min(tm, m)
    return pl.pallas_call(
        _rmsnorm_kernel,
        grid=(m // tm,),
        in_specs=[pl.BlockSpec((tm, d), lambda i: (i, 0)), pl.BlockSpec((1, d), lambda i: (0, 0))],
        out_specs=pl.BlockSpec((tm, d), lambda i: (i, 0)),
        out_shape=jax.ShapeDtypeStruct((m, d), out_dtype),
        compiler_params=_params(("parallel",)),
        name="rmsnorm",
    )(x, gain.reshape(1, d))


def _mm_kernel(a_ref, w_ref, o_ref):
    o_ref[...] = jnp.dot(a_ref[...], w_ref[...], preferred_element_type=F32).astype(o_ref.dtype)


def _mm_res_kernel(a_ref, w_ref, r_ref, o_ref):
    o_ref[...] = (r_ref[...] + jnp.dot(a_ref[...], w_ref[...], preferred_element_type=F32)).astype(o_ref.dtype)


def _mm_rotary_kernel(a_ref, w_ref, cos_ref, sin_ref, o_ref, *, dk, first_k_tile):
    acc = jnp.dot(a_ref[...], w_ref[...], preferred_element_type=F32)
    cos = cos_ref[...]
    sin = sin_ref[...]
    half = dk // 2
    scale = jnp.where(pl.program_id(1) >= first_k_tile, dk ** -0.5, 1.0)
    for lo in range(0, acc.shape[1], dk):
        x1 = acc[:, lo:lo + half]
        x2 = acc[:, lo + half:lo + dk]
        o_ref[:, lo:lo + half] = ((x1 * cos - x2 * sin) * scale).astype(o_ref.dtype)
        o_ref[:, lo + half:lo + dk] = ((x2 * cos + x1 * sin) * scale).astype(o_ref.dtype)


def _matmul(a, w, out_dtype, residual=None, name="matmul", cols=None, rotary=None):
    m, k = a.shape
    c0, n = cols if cols is not None else (0, w.shape[1])
    tm = min(m, 1024 if k <= 2048 else 512)
    tn = min(n, 1024)
    j0 = c0 // tn
    in_specs = [pl.BlockSpec((tm, k), lambda i, j: (i, 0)), pl.BlockSpec((k, tn), lambda i, j: (0, j0 + j))]
    args = [a, w]
    body = _mm_kernel
    if residual is not None:
        in_specs.append(pl.BlockSpec((tm, tn), lambda i, j: (i, j)))
        args.append(residual)
        body = _mm_res_kernel
    if rotary is not None:
        cos, sin, dk, n_q_cols = rotary
        in_specs += [pl.BlockSpec((tm, dk // 2), lambda i, j: (i, 0))] * 2
        args += [cos, sin]
        body = functools.partial(_mm_rotary_kernel, dk=dk, first_k_tile=n_q_cols // tn)
    return pl.pallas_call(
        body,
        grid=(m // tm, n // tn),
        in_specs=in_specs,
        out_specs=pl.BlockSpec((tm, tn), lambda i, j: (i, j)),
        out_shape=jax.ShapeDtypeStruct((m, n), out_dtype),
        compiler_params=_params(("parallel", "parallel")),
        name=name,
    )(*args)


def _ret_kernel(ld_ref, q_ref, k_ref, v_ref, *rest, backward, n):
    if backward:
        yf_ref, g_ref, gain_ref, o_ref, state_ref = rest
    else:
        o_ref, state_ref = rest
    h = pl.program_id(0)

    @pl.when(pl.program_id(1) == 0)
    def _():
        state_ref[...] = jnp.zeros_like(state_ref)

    ld = ld_ref[h]
    n_sub = q_ref.shape[0] // n

    i = lax.broadcasted_iota(jnp.int32, (n, n), 0)
    j = lax.broadcasted_iota(jnp.int32, (n, n), 1)
    pos = lax.broadcasted_iota(jnp.int32, (n, 1), 0).astype(F32)
    if backward:
        rel = (j - i).astype(F32)
        keep = j > i
        xi = jnp.exp(ld * (n - pos))
        zeta = jnp.exp(ld * pos)
    else:
        rel = (i - j).astype(F32)
        keep = i >= j
        xi = jnp.exp(ld * (pos + 1.0))
        zeta = jnp.exp(ld * (n - 1.0 - pos))
    decay = jnp.where(keep, jnp.exp(ld * jnp.maximum(rel, 0.0)), 0.0)
    chunk_decay = jnp.exp(ld * n)

    for s in range(n_sub):
        rows = slice((n_sub - 1 - s) * n, (n_sub - s) * n) if backward else slice(s * n, (s + 1) * n)
        q = q_ref[rows, :]
        k = k_ref[rows, :]
        v = v_ref[rows, :]
        scores = lax.dot_general(q, k, (((1,), (1,)), ((), ())), preferred_element_type=F32) * decay
        inner = jnp.dot(scores.astype(BF16), v, preferred_element_type=F32)
        state = state_ref[...]
        cross = jnp.dot((q.astype(F32) * xi).astype(BF16), state.astype(BF16), preferred_element_type=F32)
        state_ref[...] = chunk_decay * state + lax.dot_general(
            (k.astype(F32) * zeta).astype(BF16), v, (((0,), (0,)), ((), ())), preferred_element_type=F32)
        y = inner + cross
        if backward:
            y = y + yf_ref[rows, :]
            mu = jnp.mean(y, axis=-1, keepdims=True)
            d = y - mu
            var = jnp.mean(d * d, axis=-1, keepdims=True)
            yn = d * lax.rsqrt(var + GN_EPS) * gain_ref[...]
            o_ref[rows, :] = (_silu(g_ref[rows, :].astype(F32)) * yn).astype(o_ref.dtype)
        else:
            o_ref[rows, :] = y


def _retention(qk, vg, log_decay, gn_gain):
    t = qk.shape[0]
    hh = RET_HEADS
    dk = qk.shape[1] // (2 * hh)
    dv = vg.shape[1] // (2 * hh)
    c = min(RET_CHUNK, t)
    rows = min(RET_STEP_ROWS, t)
    ns = t // rows

    def specs(order):
        return [
            pl.BlockSpec(memory_space=pltpu.SMEM),
            pl.BlockSpec((rows, dk), lambda h, i: (order(i), h)),
            pl.BlockSpec((rows, dk), lambda h, i: (order(i), hh + h)),
            pl.BlockSpec((rows, dv), lambda h, i: (order(i), h)),
        ]

    fwd = lambda i: i
    bwd = lambda i: ns - 1 - i
    y_fwd = pl.pallas_call(
        functools.partial(_ret_kernel, backward=False, n=c),
        grid=(hh, ns),
        in_specs=specs(fwd),
        out_specs=pl.BlockSpec((rows, dv), lambda h, i: (i, h)),
        out_shape=jax.ShapeDtypeStruct((t, hh * dv), F32),
        scratch_shapes=[pltpu.VMEM((dk, dv), F32)],
        compiler_params=_params(("parallel", "arbitrary")),
        name="retention_fwd",
    )(log_decay[0], qk, qk, vg)
    return pl.pallas_call(
        functools.partial(_ret_kernel, backward=True, n=c),
        grid=(hh, ns),
        in_specs=specs(bwd) + [
            pl.BlockSpec((rows, dv), lambda h, i: (bwd(i), h)),
            pl.BlockSpec((rows, dv), lambda h, i: (bwd(i), hh + h)),
            pl.BlockSpec((1, dv), lambda h, i: (0, h)),
        ],
        out_specs=pl.BlockSpec((rows, dv), lambda h, i: (bwd(i), h)),
        out_shape=jax.ShapeDtypeStruct((t, hh * dv), BF16),
        scratch_shapes=[pltpu.VMEM((dk, dv), F32)],
        compiler_params=_params(("parallel", "arbitrary")),
        name="retention_bwd",
    )(log_decay[1], qk, qk, vg, y_fwd, vg, gn_gain.reshape(1, -1))


def _rotary_tables(t, dk):
    inv_freq = jnp.power(ROPE_BASE, -jnp.arange(0, dk, 2, dtype=F32) / dk)
    ang = jnp.arange(t, dtype=F32)[:, None] * inv_freq[None, :]
    return jnp.cos(ang), jnp.sin(ang)


def _xattn_kernel(h_ref, g_ref, wq_ref, kv_ref, wo_ref, o_ref):
    x = h_ref[...]
    hn = _rms(x, g_ref[...]).astype(BF16)
    q = jnp.dot(hn, wq_ref[...], preferred_element_type=F32)
    inner = XA_HEADS * XA_HEAD_DIM
    outs = []
    for hd in range(XA_HEADS):
        lo = hd * XA_HEAD_DIM
        qh = q[:, lo:lo + XA_HEAD_DIM].astype(BF16)
        kh = kv_ref[:, lo:lo + XA_HEAD_DIM]
        vh = kv_ref[:, inner + lo:inner + lo + XA_HEAD_DIM]
        s = lax.dot_general(qh, kh, (((1,), (1,)), ((), ())), preferred_element_type=F32) * (XA_HEAD_DIM ** -0.5)
        p = jnp.exp(s - jnp.max(s, axis=-1, keepdims=True))
        p = p / jnp.sum(p, axis=-1, keepdims=True)
        outs.append(jnp.dot(p.astype(BF16), vh, preferred_element_type=F32))
    o = jnp.concatenate(outs, axis=1).astype(BF16)
    o_ref[...] = x + jnp.dot(o, wo_ref[...], preferred_element_type=F32)


def _xattn(h, gain, mem_kv, w_q, w_out, tm=512):
    t, d = h.shape
    tm = min(tm, t)
    full = lambda a: pl.BlockSpec(a.shape, lambda i: (0,) * a.ndim)
    gain = gain.reshape(1, d)
    return pl.pallas_call(
        _xattn_kernel,
        grid=(t // tm,),
        in_specs=[pl.BlockSpec((tm, d), lambda i: (i, 0)), full(gain), full(w_q), full(mem_kv), full(w_out)],
        out_specs=pl.BlockSpec((tm, d), lambda i: (i, 0)),
        out_shape=jax.ShapeDtypeStruct((t, d), F32),
        compiler_params=_params(("parallel",)),
        name="mem_xattn",
    )(h, gain, w_q, mem_kv, w_out)


def _ffn_kernel(x_ref, wg_ref, wu_ref, wd_ref, r_ref, o_ref):
    @pl.when(pl.program_id(1) == 0)
    def _():
        o_ref[...] = r_ref[...]

    x = x_ref[...]
    g = jnp.dot(x, wg_ref[...], preferred_element_type=F32)
    u = jnp.dot(x, wu_ref[...], preferred_element_type=F32)
    o_ref[...] += jnp.dot((_silu(g) * u).astype(BF16), wd_ref[...], preferred_element_type=F32)


def _ffn(x, w_gate, w_up, w_down, residual, tm=512, tf=512):
    t, d = x.shape
    f = w_gate.shape[1]
    tm = min(tm, t)
    return pl.pallas_call(
        _ffn_kernel,
        grid=(t // tm, f // tf),
        in_specs=[
            pl.BlockSpec((tm, d), lambda i, j: (i, 0)),
            pl.BlockSpec((d, tf), lambda i, j: (0, j)),
            pl.BlockSpec((d, tf), lambda i, j: (0, j)),
            pl.BlockSpec((tf, d), lambda i, j: (j, 0)),
            pl.BlockSpec((tm, d), lambda i, j: (i, 0)),
        ],
        out_specs=pl.BlockSpec((tm, d), lambda i, j: (i, 0)),
        out_shape=jax.ShapeDtypeStruct((t, d), F32),
        compiler_params=_params(("parallel", "arbitrary")),
        name="swiglu",
    )(x, w_gate, w_up, w_down, residual)


def _na_bias_tables(rpb, rows):
    heads = rpb.shape[0]
    w = GRID_W
    nd = 2 * NA_WIN_ROWS - 1
    ncol = 2 * NA_WIN_COLS - 1
    qc = np.arange(w)[:, None]
    kc = np.arange(w)[None, :]
    cs = np.clip(qc - NA_WIN_COLS // 2, 0, w - NA_WIN_COLS)
    col_ok = (kc >= cs) & (kc < cs + NA_WIN_COLS)
    dc = np.clip(kc - qc + NA_WIN_COLS - 1, 0, ncol - 1)
    sel = (dc.reshape(-1)[None, :] == np.arange(ncol)[:, None]).astype(np.float32)
    tab = jnp.dot(rpb.reshape(heads * nd, ncol), jnp.asarray(sel), precision=lax.Precision.HIGHEST)
    tab = jnp.where(jnp.asarray(col_ok.reshape(-1))[None, :], tab, NEG).reshape(heads, nd, w, w)
    tab = jnp.concatenate([tab, jnp.full((heads, 1, w, w), NEG, F32)], axis=1)
    nb = rows // NA_QROWS
    variants = []
    for b in (0, 1, nb - 1):
        ks = int(np.clip(b * NA_QROWS - NA_WIN_ROWS // 2, 0, rows - NA_KROWS))
        r = b * NA_QROWS + np.arange(NA_QROWS)[:, None]
        kr = ks + np.arange(NA_KROWS)[None, :]
        rs = np.clip(r - NA_WIN_ROWS // 2, 0, rows - NA_WIN_ROWS)
        ok = (kr >= rs) & (kr < rs + NA_WIN_ROWS)
        didx = np.where(ok, kr - r + NA_WIN_ROWS - 1, nd)
        blk = tab[:, didx]
        variants.append(blk.transpose(0, 1, 3, 2, 4).reshape(heads, NA_QROWS * w, NA_KROWS * w))
    return jnp.stack(variants)


def _na_kernel(q_ref, *rest, nkb, dh, scale):
    k_refs = rest[:nkb]
    v_refs = rest[nkb:2 * nkb]
    b_ref, o_ref = rest[2 * nkb:]
    kw = k_refs[0].shape[0]
    for hd in range(q_ref.shape[1] // dh):
        cols = slice(hd * dh, (hd + 1) * dh)
        q = q_ref[:, cols]
        s = [lax.dot_general(q, k_refs[j][:, cols], (((1,), (1,)), ((), ())), preferred_element_type=F32) * scale
             + b_ref[hd, :, j * kw:(j + 1) * kw] for j in range(nkb)]
        m = functools.reduce(jnp.maximum, [jnp.max(x, axis=-1, keepdims=True) for x in s])
        p = [jnp.exp(x - m) for x in s]
        inv = 1.0 / functools.reduce(jnp.add, [jnp.sum(x, axis=-1, keepdims=True) for x in p])
        acc = None
        for j in range(nkb):
            part = jnp.dot(p[j].astype(BF16), v_refs[j][:, cols], preferred_element_type=F32)
            acc = part if acc is None else acc + part
        o_ref[:, cols] = (acc * inv).astype(o_ref.dtype)


def _neighbourhood_attention(qkv, bias):
    t = qkv.shape[0]
    heads = NA_HEADS
    dh = qkv.shape[1] // (3 * heads)
    rows = t // GRID_W
    nb = rows // NA_QROWS
    tq = NA_QROWS * GRID_W
    kw = tq // 2
    nkb = NA_KROWS * GRID_W // kw
    max_kb = (rows - NA_KROWS) * GRID_W // kw

    def kstart(b):
        return jnp.clip(2 * b - 1, 0, max_kb)

    def variant(b):
        return jnp.where(b == 0, 0, jnp.where(b == nb - 1, 2, 1))

    hps = NA_HEADS_PER_STEP
    hg = heads // hps
    kspecs = [pl.BlockSpec((kw, hps * dh), functools.partial(lambda h, b, j: (kstart(b) + j, hg + h), j=j))
              for j in range(nkb)]
    vspecs = [pl.BlockSpec((kw, hps * dh), functools.partial(lambda h, b, j: (kstart(b) + j, 2 * hg + h), j=j))
              for j in range(nkb)]
    return pl.pallas_call(
        functools.partial(_na_kernel, nkb=nkb, dh=dh, scale=dh ** -0.5),
        grid=(hg, nb),
        in_specs=[pl.BlockSpec((tq, hps * dh), lambda h, b: (b, h))] + kspecs + vspecs
        + [pl.BlockSpec((None, hps, tq, nkb * kw), lambda h, b: (variant(b), h, 0, 0))],
        out_specs=pl.BlockSpec((tq, hps * dh), lambda h, b: (b, h)),
        out_shape=jax.ShapeDtypeStruct((t, heads * dh), BF16),
        compiler_params=_params(("parallel", "arbitrary")),
        name="neighbourhood_attention",
    )(qkv, *([qkv] * (2 * nkb)), bias)


def _norm_router_kernel(x_ref, g_ref, wr_ref, hn_ref, route_ref, *, n_experts):
    hn = _rms(x_ref[...], g_ref[...])
    hn_ref[...] = hn
    logits = jnp.dot(hn, wr_ref[...], precision=lax.Precision.HIGHEST, preferred_element_type=F32)
    lane = lax.broadcasted_iota(jnp.int32, logits.shape, 1)
    lg = jnp.where(lane < n_experts, logits, -jnp.inf)
    m1 = jnp.max(lg, axis=-1, keepdims=True)
    i1 = jnp.min(jnp.where(lg == m1, lane, LANES), axis=-1, keepdims=True)
    lg2 = jnp.where(lane == i1, -jnp.inf, lg)
    m2 = jnp.max(lg2, axis=-1, keepdims=True)
    i2 = jnp.min(jnp.where(lg2 == m2, lane, LANES), axis=-1, keepdims=True)
    e = jnp.exp(m2 - m1)
    g1 = 1.0 / (1.0 + e)
    g2 = e / (1.0 + e)
    route_ref[...] = jnp.where(lane == 0, i1.astype(F32), jnp.where(lane == 1, i2.astype(F32),
                               jnp.where(lane == 2, g1, jnp.where(lane == 3, g2, 0.0))))


def _norm_router(h, gain, w_router, tm=512):
    t, d = h.shape
    tm = min(tm, t)
    n_experts = w_router.shape[1]
    wr = jnp.zeros((d, LANES), F32).at[:, :n_experts].set(w_router)
    return pl.pallas_call(
        functools.partial(_norm_router_kernel, n_experts=n_experts),
        grid=(t // tm,),
        in_specs=[pl.BlockSpec((tm, d), lambda i: (i, 0)), pl.BlockSpec((1, d), lambda i: (0, 0)),
                  pl.BlockSpec((d, LANES), lambda i: (0, 0))],
        out_specs=[pl.BlockSpec((tm, d), lambda i: (i, 0)), pl.BlockSpec((tm, LANES), lambda i: (i, 0))],
        out_shape=[jax.ShapeDtypeStruct((t, d), F32), jax.ShapeDtypeStruct((t, LANES), F32)],
        compiler_params=_params(("parallel",)),
        name="norm_router",
    )(h, gain.reshape(1, d), wr)


def _row_copy(src_hbm, dst_vmem, sem, src_row, dst_row):
    return pltpu.make_async_copy(src_hbm.at[pl.ds(src_row, 1)], dst_vmem.at[pl.ds(dst_row, 1)], sem)


def _gather_kernel(tok_ref, x_hbm, o_ref, sem):
    rows = o_ref.shape[0]

    def issue(i, carry):
        _row_copy(x_hbm, o_ref, sem, tok_ref[0, 0, 2 * i], 2 * i).start(priority=0)
        _row_copy(x_hbm, o_ref, sem, tok_ref[0, 0, 2 * i + 1], 2 * i + 1).start(priority=1)
        return carry

    lax.fori_loop(0, rows // 2, issue, 0)
    pltpu.make_async_copy(x_hbm.at[pl.ds(0, rows)], o_ref, sem).wait()


def _gather_rows(x, tok_sorted):
    n_pad = tok_sorted.shape[0]
    d = x.shape[1]
    r = GATHER_ROWS
    return pl.pallas_call(
        _gather_kernel,
        grid=(n_pad // r,),
        in_specs=[pl.BlockSpec((1, 1, r), lambda i: (i, 0, 0), memory_space=pltpu.SMEM),
                  pl.BlockSpec(memory_space=pl.ANY)],
        out_specs=pl.BlockSpec((r, d), lambda i: (i, 0)),
        out_shape=jax.ShapeDtypeStruct((n_pad, d), F32),
        scratch_shapes=[pltpu.SemaphoreType.DMA(())],
        compiler_params=_params(("arbitrary",)),
        name="moe_gather",
    )(tok_sorted.reshape(n_pad // r, 1, r), x)


def _expert_kernel(be_ref, nv_ref, x_ref, wg_ref, wu_ref, wd_ref, o_ref, xb_ref):
    m = pl.program_id(0)
    f = pl.program_id(1)

    @pl.when(f == 0)
    def _():
        xb_ref[...] = x_ref[...].astype(BF16)
        o_ref[...] = jnp.zeros_like(o_ref)

    @pl.when(m < nv_ref[0])
    def _():
        x = xb_ref[...]
        g = jnp.dot(x, wg_ref[...], preferred_element_type=F32)
        u = jnp.dot(x, wu_ref[...], preferred_element_type=F32)
        o_ref[...] += jnp.dot((_silu(g) * u).astype(BF16), wd_ref[...], preferred_element_type=F32)


def _experts(xs, block_e, n_valid, w_gate, w_up, w_down, tf=1024):
    n_pad, d = xs.shape
    fdim = w_gate.shape[2]
    tm = MOE_ROWS
    nb = n_pad // tm
    nf = fdim // tf

    def fidx(m, f, nv):
        return jnp.where(m < nv[0], f, nf - 1)

    grid_spec = pltpu.PrefetchScalarGridSpec(
        num_scalar_prefetch=2,
        grid=(nb, nf),
        in_specs=[
            pl.BlockSpec((tm, d), lambda m, f, be, nv: (jnp.minimum(m, nv[0] - 1), 0)),
            pl.BlockSpec((None, d, tf), lambda m, f, be, nv: (be[m], 0, fidx(m, f, nv))),
            pl.BlockSpec((None, d, tf), lambda m, f, be, nv: (be[m], 0, fidx(m, f, nv))),
            pl.BlockSpec((None, tf, d), lambda m, f, be, nv: (be[m], fidx(m, f, nv), 0)),
        ],
        out_specs=pl.BlockSpec((tm, d), lambda m, f, be, nv: (m, 0)),
        scratch_shapes=[pltpu.VMEM((tm, d), BF16)],
    )
    return pl.pallas_call(
        _expert_kernel,
        grid_spec=grid_spec,
        out_shape=jax.ShapeDtypeStruct((n_pad, d), F32),
        compiler_params=_params(("arbitrary", "arbitrary")),
        name="moe_experts",
    )(block_e, n_valid, xs, w_gate, w_up, w_down)


def _combine_kernel(pos_ref, h_ref, route_ref, gain_ref, y_hbm, o_ref, y0_ref, y1_ref, sem, *, final_norm):
    rows = h_ref.shape[0]

    def issue(i, carry):
        _row_copy(y_hbm, y0_ref, sem.at[0], pos_ref[0, 0, 2 * i], i).start(priority=0)
        _row_copy(y_hbm, y1_ref, sem.at[1], pos_ref[0, 0, 2 * i + 1], i).start(priority=1)
        return carry

    lax.fori_loop(0, rows, issue, 0)
    pltpu.make_async_copy(y_hbm.at[pl.ds(0, rows)], y0_ref, sem.at[0]).wait()
    pltpu.make_async_copy(y_hbm.at[pl.ds(0, rows)], y1_ref, sem.at[1]).wait()
    g1 = route_ref[:, 2:3]
    g2 = route_ref[:, 3:4]
    hh = h_ref[...] + (g1 * y0_ref[...] + g2 * y1_ref[...])
    o_ref[...] = _rms(hh, gain_ref[...]) if final_norm else hh


def _combine(h, y, pos, route, final_gain, final_norm):
    t, d = h.shape
    r = GATHER_ROWS
    return pl.pallas_call(
        functools.partial(_combine_kernel, final_norm=final_norm),
        grid=(t // r,),
        in_specs=[pl.BlockSpec((1, 1, MOE_TOP_K * r), lambda i: (i, 0, 0), memory_space=pltpu.SMEM),
                  pl.BlockSpec((r, d), lambda i: (i, 0)),
                  pl.BlockSpec((r, LANES), lambda i: (i, 0)),
                  pl.BlockSpec((1, d), lambda i: (0, 0)),
                  pl.BlockSpec(memory_space=pl.ANY)],
        out_specs=pl.BlockSpec((r, d), lambda i: (i, 0)),
        out_shape=jax.ShapeDtypeStruct((t, d), F32),
        scratch_shapes=[pltpu.VMEM((r, d), F32), pltpu.VMEM((r, d), F32), pltpu.SemaphoreType.DMA((2,))],
        compiler_params=_params(("arbitrary",)),
        name="moe_combine_norm",
    )(pos.reshape(t // r, 1, MOE_TOP_K * r), h, route, final_gain.reshape(1, d), y)


def _routing_plan(route, n_experts):
    t = route.shape[0]
    n_assign = t * MOE_TOP_K
    e_flat = route[:, :MOE_TOP_K].astype(jnp.int32).reshape(-1)
    onehot = (e_flat[:, None] == jnp.arange(n_experts, dtype=jnp.int32)[None, :]).astype(jnp.int32)
    csum = jnp.cumsum(onehot, axis=0)
    rank = jnp.take_along_axis(csum, e_flat[:, None], axis=1)[:, 0] - 1
    counts = csum[-1]
    padded = (counts + MOE_ROWS - 1) // MOE_ROWS * MOE_ROWS
    pend = jnp.cumsum(padded)
    pos = (pend - padded)[e_flat] + rank
    n_blocks = n_assign // MOE_ROWS + n_experts
    tok = jnp.arange(n_assign, dtype=jnp.int32) // MOE_TOP_K
    tok_sorted = jnp.zeros((n_blocks * MOE_ROWS,), jnp.int32).at[pos].set(tok)
    n_valid = (pend[-1] // MOE_ROWS).astype(jnp.int32)
    blk = jnp.arange(n_blocks, dtype=jnp.int32)
    block_e = jnp.sum((blk * MOE_ROWS)[:, None] >= pend[None, :], axis=1).astype(jnp.int32)
    block_e = jnp.where(blk < n_valid, block_e, block_e[n_valid - 1])
    return pos.astype(jnp.int32), tok_sorted, block_e, n_valid.reshape(1)


def kernel(x, mem, mem_norm_gain, mem_w_kv, norm_gain, ret_w_in, ret_w_out, ret_decay_logit, ret_gn_gain, ffn_w_gate, ffn_w_up, ffn_w_down, na_w_in, na_w_out, na_rpb, moe_router, moe_w_gate, moe_w_up, moe_w_down, xa_w_q, xa_w_out, final_norm_gain):
    b, s, d = x.shape
    assert b == 1
    depth = norm_gain.shape[0]
    bf = lambda w: w.astype(BF16)

    mem_kv = _matmul(_rmsnorm(mem[0], mem_norm_gain, BF16), bf(mem_w_kv), BF16, name="mem_kv")
    dk = (ret_w_in.shape[2] - 2 * ret_gn_gain.shape[1]) // (2 * RET_HEADS)
    cos, sin = _rotary_tables(s, dk)

    h = x[0]
    for i in range(depth):
        j = i // 2
        hn = _rmsnorm(h, norm_gain[i, 0], BF16)
        if i % 2 == 0:
            w_in = bf(ret_w_in[j])
            n_qk = 2 * RET_HEADS * dk
            qk = _matmul(hn, w_in, BF16, name="ret_in_qk", cols=(0, n_qk), rotary=(cos, sin, dk, n_qk // 2))
            vg = _matmul(hn, w_in, BF16, name="ret_in_vg", cols=(n_qk, w_in.shape[1] - n_qk))
            log_decay = jax.nn.log_sigmoid(ret_decay_logit[j].astype(F32))
            yg = _retention(qk, vg, log_decay, ret_gn_gain[j])
            h = _matmul(yg, bf(ret_w_out[j]), F32, residual=h, name="ret_out")
        else:
            qkv = _matmul(hn, bf(na_w_in[j]), BF16, name="na_in")
            o = _neighbourhood_attention(qkv, _na_bias_tables(na_rpb[j], s // GRID_W))
            h = _matmul(o, bf(na_w_out[j]), F32, residual=h, name="na_out")
        h = _xattn(h, norm_gain[i, 1], mem_kv, bf(xa_w_q[i]), bf(xa_w_out[i]))
        if i % 2 == 0:
            hn = _rmsnorm(h, norm_gain[i, 2], BF16)
            h = _ffn(hn, bf(ffn_w_gate[j]), bf(ffn_w_up[j]), bf(ffn_w_down[j]), h)
        else:
            hn32, route = _norm_router(h, norm_gain[i, 2], moe_router[j])
            pos, tok_sorted, block_e, n_valid = _routing_plan(route, moe_router.shape[2])
            xs = _gather_rows(hn32, tok_sorted)
            y = _experts(xs, block_e, n_valid, bf(moe_w_gate[j]), bf(moe_w_up[j]), bf(moe_w_down[j]))
            h = _combine(h, y, pos, route, final_norm_gain, final_norm=(i == depth - 1))
    if depth % 2 == 1:
        h = _rmsnorm(h, final_norm_gain, F32)
    return h[None]
```
